```python
import math
import jax, jax.numpy as jnp
from jax import lax
import numpy as np

D_MODEL = 2048
BATCH = 1
SEQ = 8192
DEPTH = 1

MIX_WIDTH = D_MODEL
DIFF_HEADS = 8
DIFF_HEAD_DIM = 128
DIFF_QK_DIM = DIFF_HEAD_DIM // 2
SB_HEADS = 8
SB_HEAD_DIM = 128
DIFF_WIDTH = DIFF_HEADS * DIFF_HEAD_DIM
SB_WIDTH = SB_HEADS * SB_HEAD_DIM
IN_COLS = 3 * DIFF_WIDTH + 3 * SB_WIDTH
ROPE_THETA = 500000.0
ROT_DIM = DIFF_QK_DIM // 4
Q_BLOCK = 128
PEER_HEADS = 8
PEER_NKEYS = 128
PEER_N = PEER_NKEYS * PEER_NKEYS
PEER_QDIM = 256
PEER_HALF = PEER_QDIM // 2
PEER_TOPK = 16
PLE_DIM = 256
LN_EPS = 1e-5

kernel_name = "hymba_diff_stickbreak_peer_deepnorm"


def layer_norm(x, g, b):
    xf = x.astype(jnp.float32)
    mu = jnp.mean(xf, axis=-1, keepdims=True)
    xc = xf - mu
    var = jnp.mean(xc * xc, axis=-1, keepdims=True)
    return (xc * lax.rsqrt(var + LN_EPS) * g.astype(jnp.float32) + b.astype(jnp.float32)).astype(x.dtype)


def rope_tables(seq):
    pos = jnp.arange(seq, dtype=jnp.float32)
    inv = ROPE_THETA ** (-jnp.arange(0, ROT_DIM, 2, dtype=jnp.float32) / ROT_DIM)
    ang = pos[:, None] * inv[None, :]
    return jnp.cos(ang), jnp.sin(ang)


def apply_partial_rope(t, cos, sin):
    half = ROT_DIM // 2
    a = t[..., :half]
    b = t[..., half:ROT_DIM]
    c = cos.astype(t.dtype)
    s = sin.astype(t.dtype)
    rot = jnp.concatenate([a * c - b * s, b * c + a * s], axis=-1)
    return jnp.concatenate([rot, t[..., ROT_DIM:]], axis=-1)


def split_heads(t, n_heads):
    b, s, _ = t.shape
    return t.reshape(b, s, n_heads, -1).transpose(0, 2, 1, 3)


def merge_heads(t):
    b, h, s, d = t.shape
    return t.transpose(0, 2, 1, 3).reshape(b, s, h * d)


def to_blocks(t):
    b, h, s, d = t.shape
    nb = s // Q_BLOCK
    return t.reshape(b, h, nb, Q_BLOCK, d).transpose(2, 0, 1, 3, 4)


def from_blocks(o):
    nb, b, h, qb, d = o.shape
    return o.transpose(1, 2, 0, 3, 4).reshape(b, h, nb * qb, d)


def diff_attention(q1, q2, k1, k2, v, lam):
    seq = k1.shape[2]
    nb = seq // Q_BLOCK
    scale = DIFF_QK_DIM ** -0.5
    kpos = jnp.arange(seq)
    k1f = k1.astype(jnp.float32)
    k2f = k2.astype(jnp.float32)
    vf = v.astype(jnp.float32)

    def block(args):
        q1b, q2b, bi = args
        qpos = bi * Q_BLOCK + jnp.arange(Q_BLOCK)
        mask = kpos[None, :] <= qpos[:, None]
        s1 = jnp.einsum('bhqd,bhkd->bhqk', q1b.astype(jnp.float32), k1f) * scale
        s2 = jnp.einsum('bhqd,bhkd->bhqk', q2b.astype(jnp.float32), k2f) * scale
        p1 = jax.nn.softmax(jnp.where(mask, s1, -jnp.inf), axis=-1)
        p2 = jax.nn.softmax(jnp.where(mask, s2, -jnp.inf), axis=-1)
        return jnp.einsum('bhqk,bhkd->bhqd', p1 - lam * p2, vf)

    out = lax.map(block, (to_blocks(q1), to_blocks(q2), jnp.arange(nb)))
    return from_blocks(out)


def stick_breaking_attention(q, k, v):
    seq = k.shape[2]
    nb = seq // Q_BLOCK
    scale = SB_HEAD_DIM ** -0.5
    kpos = jnp.arange(seq)
    kf = k.astype(jnp.float32)
    vf = v.astype(jnp.float32)

    def block(args):
        qb, bi = args
        qpos = bi * Q_BLOCK + jnp.arange(Q_BLOCK)
        valid = kpos[None, :] < qpos[:, None]
        z = jnp.einsum('bhqd,bhkd->bhqk', qb.astype(jnp.float32), kf) * scale
        log_beta = jax.nn.log_sigmoid(z)
        log_keep = jnp.where(valid, jax.nn.log_sigmoid(-z), 0.0)
        suffix = lax.cumsum(log_keep, axis=3, reverse=True) - log_keep
        a = jnp.where(valid, jnp.exp(log_beta + suffix), 0.0)
        return jnp.einsum('bhqk,bhkd->bhqd', a, vf)

    out = lax.map(block, (to_blocks(q), jnp.arange(nb)))
    return from_blocks(out)


def peer_ffn(x, wq, sub_keys, u, v):
    b, s, d = x.shape
    t = b * s
    xt = x.reshape(t, d)
    q = (xt @ wq).reshape(t, PEER_HEADS, PEER_QDIM)
    qa, qb = q[..., :PEER_HALF], q[..., PEER_HALF:]
    sa = jnp.einsum('thc,nc->thn', qa, sub_keys[0]).astype(jnp.float32)
    sb = jnp.einsum('thc,nc->thn', qb, sub_keys[1]).astype(jnp.float32)
    va, ia = lax.top_k(sa, PEER_TOPK)
    vb, ib = lax.top_k(sb, PEER_TOPK)
    cand = (va[..., :, None] + vb[..., None, :]).reshape(t, PEER_HEADS, PEER_TOPK * PEER_TOPK)
    cidx = (ia[..., :, None] * PEER_NKEYS + ib[..., None, :]).reshape(t, PEER_HEADS, PEER_TOPK * PEER_TOPK)
    top, sel = lax.top_k(cand, PEER_TOPK)
    eidx = jnp.take_along_axis(cidx, sel, axis=-1)
    gates = jax.nn.softmax(top, axis=-1)
    nb = t // Q_BLOCK

    def block(args):
        xb, eb, gb = args
        ub = u[eb]
        vbk = v[eb]
        act = jax.nn.gelu(jnp.einsum('thed,td->the', ub, xb).astype(jnp.float32))
        w = (gb * act).astype(xb.dtype)
        return jnp.einsum('the,thed->td', w, vbk)

    out = lax.map(block, (xt.reshape(nb, Q_BLOCK, d),
                          eidx.reshape(nb, Q_BLOCK, PEER_HEADS, PEER_TOPK),
                          gates.reshape(nb, Q_BLOCK, PEER_HEADS, PEER_TOPK)))
    return out.reshape(b, s, d)


def setup_inputs(seed: int = 0) -> dict:
    key = jax.random.key(seed)
    ks = jax.random.split(key, 16)
    f32 = jnp.float32
    beta = (8.0 * DEPTH) ** -0.25
    x = jax.random.normal(ks[0], (BATCH, SEQ, D_MODEL), f32)
    p = jax.random.normal(ks[1], (DEPTH, BATCH, SEQ, PLE_DIM), f32)
    col_scale = jnp.ones((IN_COLS,), f32)
    col_scale = col_scale.at[2 * DIFF_WIDTH:3 * DIFF_WIDTH].set(beta)
    col_scale = col_scale.at[3 * DIFF_WIDTH + 2 * SB_WIDTH:].set(beta)
    w_in = jax.random.normal(ks[2], (DEPTH, D_MODEL, IN_COLS), f32) * (D_MODEL ** -0.5) * col_scale
    w_out = jax.random.normal(ks[3], (DEPTH, MIX_WIDTH, D_MODEL), f32) * (MIX_WIDTH ** -0.5) * beta
    lam = jax.random.normal(ks[4], (4, DEPTH, DIFF_QK_DIM), f32) * 0.1
    diff_subln_g = 1.0 + 0.01 * jax.random.normal(ks[5], (DEPTH, DIFF_HEAD_DIM), f32)
    ln_g = 1.0 + 0.01 * jax.random.normal(ks[6], (DEPTH, 3, D_MODEL), f32)
    ln_b = 0.01 * jax.random.normal(ks[7], (DEPTH, 3, D_MODEL), f32)
    peer_wq = jax.random.normal(ks[8], (DEPTH, D_MODEL, PEER_HEADS * PEER_QDIM), f32) * (D_MODEL ** -0.5)
    peer_keys = jax.random.normal(ks[9], (DEPTH, 2, PEER_NKEYS, PEER_HALF), f32) * (PEER_HALF ** -0.5)
    peer_u = jax.random.normal(ks[10], (DEPTH, PEER_N, D_MODEL), f32) * (D_MODEL ** -0.5)
    peer_v = jax.random.normal(ks[11], (DEPTH, PEER_N, D_MODEL), f32) * (PEER_HEADS ** -0.5) * beta
    ple_gate = jax.random.normal(ks[12], (DEPTH, D_MODEL, D_MODEL), f32) * (D_MODEL ** -0.5)
    ple_proj = jax.random.normal(ks[13], (DEPTH, PLE_DIM, D_MODEL), f32) * (PLE_DIM ** -0.5) * beta
    return {"x": x, "p": p, "w_in": w_in, "w_out": w_out,
            "lambda_q1": lam[0], "lambda_k1": lam[1], "lambda_q2": lam[2], "lambda_k2": lam[3],
            "diff_subln_g": diff_subln_g, "ln_g": ln_g, "ln_b": ln_b,
            "peer_wq": peer_wq, "peer_keys": peer_keys, "peer_u": peer_u, "peer_v": peer_v,
            "ple_gate": ple_gate, "ple_proj": ple_proj}


def reference(x, p, w_in, w_out, lambda_q1, lambda_k1, lambda_q2, lambda_k2, diff_subln_g,
              ln_g, ln_b, peer_wq, peer_keys, peer_u, peer_v, ple_gate, ple_proj):
    alpha = (2.0 * DEPTH) ** 0.25
    seq = x.shape[1]
    cos, sin = rope_tables(seq)
    splits = [DIFF_WIDTH, 2 * DIFF_WIDTH, 3 * DIFF_WIDTH,
              3 * DIFF_WIDTH + SB_WIDTH, 3 * DIFF_WIDTH + 2 * SB_WIDTH]
    for i in range(DEPTH):
        proj = jnp.einsum('bsd,dc->bsc', x, w_in[i])
        dq, dk, dv, sq, sk, sv = jnp.split(proj, splits, axis=-1)
        dq = split_heads(dq, DIFF_HEADS)
        dk = split_heads(dk, DIFF_HEADS)
        dv = split_heads(dv, DIFF_HEADS)
        q1 = apply_partial_rope(dq[..., :DIFF_QK_DIM], cos, sin)
        q2 = apply_partial_rope(dq[..., DIFF_QK_DIM:], cos, sin)
        k1 = apply_partial_rope(dk[..., :DIFF_QK_DIM], cos, sin)
        k2 = apply_partial_rope(dk[..., DIFF_QK_DIM:], cos, sin)
        lambda_init = 0.8 - 0.6 * math.exp(-0.3 * i)
        lam = (jnp.exp(jnp.sum(lambda_q1[i].astype(jnp.float32) * lambda_k1[i].astype(jnp.float32)))
               - jnp.exp(jnp.sum(lambda_q2[i].astype(jnp.float32) * lambda_k2[i].astype(jnp.float32)))
               + lambda_init)
        od = diff_attention(q1, q2, k1, k2, dv, lam)
        od = od * lax.rsqrt(jnp.mean(od * od, axis=-1, keepdims=True) + LN_EPS)
        od = od * diff_subln_g[i].astype(jnp.float32) * (1.0 - lambda_init)
        osb = stick_breaking_attention(split_heads(sq, SB_HEADS), split_heads(sk, SB_HEADS),
                                       split_heads(sv, SB_HEADS))
        merged = jnp.concatenate([merge_heads(od), merge_heads(osb)], axis=-1).astype(x.dtype)
        mix_out = jnp.einsum('bsc,cd->bsd', merged, w_out[i])
        x = layer_norm(alpha * x + mix_out, ln_g[i, 0], ln_b[i, 0])
        ffn_out = peer_ffn(x, peer_wq[i], peer_keys[i], peer_u[i], peer_v[i])
        x = layer_norm(alpha * x + ffn_out, ln_g[i, 1], ln_b[i, 1])
        gate = jax.nn.sigmoid(jnp.einsum('bsd,de->bse', x, ple_gate[i]))
        emb = jnp.einsum('bsk,kd->bsd', p[i], ple_proj[i])
        x = layer_norm(alpha * x + gate * emb, ln_g[i, 2], ln_b[i, 2])
    return x
```

```python
import functools
import math

import jax
import jax.numpy as jnp
from jax import lax
from jax.experimental import pallas as pl
from jax.experimental.pallas import tpu as pltpu

F32 = jnp.float32
BF16 = jnp.bfloat16

LANES = 128
V7X_VMEM_BYTES = 64 * 1024 * 1024
VMEM_LIMIT = V7X_VMEM_BYTES * 3 // 4

DIFF_HEADS = 8
SB_HEADS = 8
HEAD_DIM = 128
DIFF_QK_DIM = HEAD_DIM // 2
ROT_DIM = DIFF_QK_DIM // 4
ROPE_THETA = 500000.0
PEER_HEADS = 8
PEER_NKEYS = 128
PEER_HALF = 128
PEER_TOPK = 16
LN_EPS = 1e-5
DEPTH = 1

NT_DIMS = (((1,), (1,)), ((), ()))


def _params(n_grid):
    return pltpu.CompilerParams(dimension_semantics=("arbitrary",) * n_grid,
                                vmem_limit_bytes=VMEM_LIMIT)


def _layer_norm(y, g, b):
    mu = jnp.mean(y, axis=-1, keepdims=True)
    yc = y - mu
    var = jnp.mean(yc * yc, axis=-1, keepdims=True)
    return yc * lax.rsqrt(var + LN_EPS) * g + b


def _in_proj_kernel(x_ref, w_ref, c_ref, s1_ref, s2_ref, o_ref, *, n_rope_tiles):
    j = pl.program_id(0)
    acc = jnp.dot(x_ref[...], w_ref[...], preferred_element_type=F32)

    @pl.when(j >= n_rope_tiles)
    def _():
        o_ref[...] = acc.astype(o_ref.dtype)

    @pl.when(j < n_rope_tiles)
    def _():
        c = c_ref[0]
        s1 = s1_ref[0]
        s2 = s2_ref[0]
        for hb in range(acc.shape[1] // LANES):
            t = acc[:, hb * LANES:(hb + 1) * LANES]
            half = ROT_DIM // 2
            r = (t * c + pltpu.roll(t, half, axis=1) * s1
                 + pltpu.roll(t, LANES - half, axis=1) * s2)
            o_ref[:, hb * LANES:(hb + 1) * LANES] = r.astype(o_ref.dtype)


def _rope_lane_tables(seq):
    half = ROT_DIM // 2
    pos = jnp.arange(seq, dtype=F32)
    inv = ROPE_THETA ** (-jnp.arange(0, ROT_DIM, 2, dtype=F32) / ROT_DIM)
    ang = pos[:, None] * inv[None, :]
    cos, sin = jnp.cos(ang), jnp.sin(ang)
    ones = jnp.ones((seq, DIFF_QK_DIM - ROT_DIM), F32)
    zeros_h = jnp.zeros((seq, half), F32)
    zeros_r = jnp.zeros((seq, DIFF_QK_DIM - ROT_DIM), F32)
    c64 = jnp.concatenate([cos, cos, ones], axis=1)
    s1_64 = jnp.concatenate([zeros_h, sin, zeros_r], axis=1)
    s2_64 = jnp.concatenate([-sin, zeros_h, zeros_r], axis=1)
    tabs = [jnp.concatenate([t, t], axis=1) for t in (c64, s1_64, s2_64)]
    qscale = DIFF_QK_DIM ** -0.5
    return [jnp.stack([t * qscale, t]) for t in tabs]


def _in_proj(xb, w_in_b, seq):
    d_model = xb.shape[1]
    n_cols = w_in_b.shape[1]
    tn = DIFF_HEADS * HEAD_DIM
    tm = min(1024, seq)
    c_tab, s1_tab, s2_tab = _rope_lane_tables(seq)
    tab_spec = pl.BlockSpec((1, tm, LANES), lambda j, i: (jnp.minimum(j, 1), i, 0))
    return pl.pallas_call(
        functools.partial(_in_proj_kernel, n_rope_tiles=2),
        grid=(n_cols // tn, seq // tm),
        in_specs=[pl.BlockSpec((tm, d_model), lambda j, i: (i, 0)),
                  pl.BlockSpec((d_model, tn), lambda j, i: (0, j)),
                  tab_spec, tab_spec, tab_spec],
        out_specs=pl.BlockSpec((tm, tn), lambda j, i: (i, j)),
        out_shape=jax.ShapeDtypeStruct((seq, n_cols), BF16),
        compiler_params=_params(2),
        name="in_proj",
    )(xb, w_in_b, c_tab, s1_tab, s2_tab)


def _diff_attn_kernel(q_ref, k_ref, v_ref, lq1_ref, lk1_ref, lq2_ref, lk2_ref, g_ref, o_ref,
                      *, t, lambda_init):
    qi = pl.program_id(1)
    q = q_ref[...]
    lane = lax.broadcasted_iota(jnp.int32, q.shape, 1)
    zero = jnp.zeros_like(q)
    q1 = jnp.where(lane < DIFF_QK_DIM, q, zero)
    q2 = jnp.where(lane < DIFF_QK_DIM, zero, q)

    def scores(ki):
        start = pl.multiple_of(ki * t, t)
        k = k_ref[pl.ds(start, t), :]
        v = v_ref[pl.ds(start, t), :]
        s1 = lax.dot_general(q1, k, NT_DIMS, preferred_element_type=F32)
        s2 = lax.dot_general(q2, k, NT_DIMS, preferred_element_type=F32)
        return s1, s2, v

    def first(s, v, mask):
        s = jnp.where(mask, s, -jnp.inf)
        m = jnp.max(s, axis=-1, keepdims=True)
        p = jnp.exp(s - m)
        l = jnp.sum(p, axis=-1, keepdims=True)
        acc = jnp.dot(p.astype(BF16), v, preferred_element_type=F32)
        return m, l, acc

    def update(s, v, m, l, acc):
        m_new = jnp.maximum(m, jnp.max(s, axis=-1, keepdims=True))
        a = jnp.exp(m - m_new)
        p = jnp.exp(s - m_new)
        l = a * l + jnp.sum(p, axis=-1, keepdims=True)
        acc = a * acc + jnp.dot(p.astype(BF16), v, preferred_element_type=F32)
        return m_new, l, acc

    row = lax.broadcasted_iota(jnp.int32, (t, t), 0)
    col = lax.broadcasted_iota(jnp.int32, (t, t), 1)
    s1, s2, v = scores(qi)
    st1 = first(s1, v, col <= row)
    st2 = first(s2, v, col <= row)

    def body(ki, carry):
        c1, c2 = carry
        s1, s2, v = scores(ki)
        return update(s1, v, *c1), update(s2, v, *c2)

    (m1, l1, a1), (m2, l2, a2) = lax.fori_loop(0, qi, body, (st1, st2))

    lam = (jnp.exp(jnp.sum(lq1_ref[...] * lk1_ref[...], axis=-1, keepdims=True))
           - jnp.exp(jnp.sum(lq2_ref[...] * lk2_ref[...], axis=-1, keepdims=True))
           + lambda_init)
    od = a1 / l1 - lam * (a2 / l2)
    od = od * lax.rsqrt(jnp.mean(od * od, axis=-1, keepdims=True) + LN_EPS)
    od = od * g_ref[...] * (1.0 - lambda_init)
    o_ref[...] = od.astype(o_ref.dtype)


def _diff_attn(proj, lq1, lk1, lq2, lk2, g, seq, layer):
    t = min(256, seq)
    lambda_init = 0.8 - 0.6 * math.exp(-0.3 * layer)
    vec = pl.BlockSpec((1, DIFF_QK_DIM), lambda h, i: (0, 0))
    return pl.pallas_call(
        functools.partial(_diff_attn_kernel, t=t, lambda_init=lambda_init),
        grid=(DIFF_HEADS, seq // t),
        in_specs=[pl.BlockSpec((t, HEAD_DIM), lambda h, i: (i, h)),
                  pl.BlockSpec((seq, HEAD_DIM), lambda h, i: (0, DIFF_HEADS + h)),
                  pl.BlockSpec((seq, HEAD_DIM), lambda h, i: (0, 2 * DIFF_HEADS + h)),
                  vec, vec, vec, vec,
                  pl.BlockSpec((1, HEAD_DIM), lambda h, i: (0, 0))],
        out_specs=pl.BlockSpec((t, HEAD_DIM), lambda h, i: (i, h)),
        out_shape=jax.ShapeDtypeStruct((seq, DIFF_HEADS * HEAD_DIM), BF16),
        compiler_params=_params(2),
        name="diff_attn",
    )(proj, proj, proj, lq1, lk1, lq2, lk2, g)


def _sb_attn_kernel(q_ref, k_ref, v_ref, o_ref, *, t, scale):
    qi = pl.program_id(1)
    q = q_ref[...]
    row = lax.broadcasted_iota(jnp.int32, (t, t), 0)
    col = lax.broadcasted_iota(jnp.int32, (t, t), 1)
    tri = jnp.where(row > col, 1.0, 0.0).astype(BF16)
    tri2 = jnp.concatenate([tri, tri], axis=0)

    def chunk(ki, run, acc, valid):
        start = pl.multiple_of(ki * t, t)
        k = k_ref[pl.ds(start, t), :]
        v = v_ref[pl.ds(start, t), :]
        z = lax.dot_general(q, k, NT_DIMS, preferred_element_type=F32) * scale
        sp = jnp.log1p(jnp.exp(-jnp.abs(z)))
        log_beta = jnp.minimum(z, 0.0) - sp
        log_keep = -jnp.maximum(z, 0.0) - sp
        if valid is not None:
            log_keep = jnp.where(valid, log_keep, 0.0)
        hi = log_keep.astype(BF16)
        lo = (log_keep - hi.astype(F32)).astype(BF16)
        suffix = jnp.dot(jnp.concatenate([hi, lo], axis=1), tri2, preferred_element_type=F32)
        a = jnp.exp(log_beta + suffix + run)
        if valid is not None:
            a = jnp.where(valid, a, 0.0)
        acc = acc + jnp.dot(a.astype(BF16), v, preferred_element_type=F32)
        run = run + jnp.sum(log_keep, axis=-1, keepdims=True)
        return run, acc

    run0 = jnp.zeros((t, 1), F32)
    acc0 = jnp.zeros((t, HEAD_DIM), F32)
    run, acc = chunk(qi, run0, acc0, col < row)

    def body(step, carry):
        return chunk(qi - 1 - step, carry[0], carry[1], None)

    run, acc = lax.fori_loop(0, qi, body, (run, acc))
    o_ref[...] = acc.astype(o_ref.dtype)


def _sb_attn(proj, seq):
    t = min(256, seq)
    base = 3 * DIFF_HEADS
    return pl.pallas_call(
        functools.partial(_sb_attn_kernel, t=t, scale=HEAD_DIM ** -0.5),
        grid=(SB_HEADS, seq // t),
        in_specs=[pl.BlockSpec((t, HEAD_DIM), lambda h, i: (i, base + h)),
                  pl.BlockSpec((seq, HEAD_DIM), lambda h, i: (0, base + SB_HEADS + h)),
                  pl.BlockSpec((seq, HEAD_DIM), lambda h, i: (0, base + 2 * SB_HEADS + h))],
        out_specs=pl.BlockSpec((t, HEAD_DIM), lambda h, i: (i, h)),
        out_shape=jax.ShapeDtypeStruct((seq, SB_HEADS * HEAD_DIM), BF16),
        compiler_params=_params(2),
        name="sb_attn",
    )(proj, proj, proj)


def _out_proj_kernel(od_ref, osb_ref, w_ref, x_ref, g_ref, b_ref, o_ref, ot_ref, *, alpha):
    n_diff = od_ref.shape[1]
    y = jnp.dot(od_ref[...], w_ref[:n_diff, :], preferred_element_type=F32)
    y = y + jnp.dot(osb_ref[...], w_ref[n_diff:, :], preferred_element_type=F32)
    y = y + alpha * x_ref[...]
    out = _layer_norm(y, g_ref[...], b_ref[...])
    o_ref[...] = out
    ot_ref[...] = out.T.astype(ot_ref.dtype)


def _out_proj(od, osb, w_out_b, x2d, g, b, alpha):
    seq, d_model = x2d.shape
    tm = min(512, seq)
    row = pl.BlockSpec((1, d_model), lambda i: (0, 0))
    return pl.pallas_call(
        functools.partial(_out_proj_kernel, alpha=alpha),
        grid=(seq // tm,),
        in_specs=[pl.BlockSpec((tm, od.shape[1]), lambda i: (i, 0)),
                  pl.BlockSpec((tm, osb.shape[1]), lambda i: (i, 0)),
                  pl.BlockSpec(w_out_b.shape, lambda i: (0, 0)),
                  pl.BlockSpec((tm, d_model), lambda i: (i, 0)),
                  row, row],
        out_specs=[pl.BlockSpec((tm, d_model), lambda i: (i, 0)),
                   pl.BlockSpec((d_model, tm), lambda i: (0, i))],
        out_shape=[jax.ShapeDtypeStruct((seq, d_model), F32),
                   jax.ShapeDtypeStruct((d_model, seq), BF16)],
        compiler_params=_params(1),
        name="out_proj",
    )(od, osb, w_out_b, x2d, g, b)


def _top_values(s, k):
    out = []
    for _ in range(k):
        m = jnp.max(s, axis=0, keepdims=True)
        out.append(m)
        s = jnp.where(s == m, -jnp.inf, s)
    return out


def _peer_route_kernel(x_ref, wq_ref, keys_ref, sb_ref, cut_ref, ea_ref, ebz_ref):
    q = jnp.dot(x_ref[...].astype(BF16), wq_ref[...], preferred_element_type=F32).astype(BF16)
    inf = jnp.inf
    for h in range(PEER_HEADS):
        qa = q[:, (2 * h) * PEER_HALF:(2 * h + 1) * PEER_HALF]
        qb = q[:, (2 * h + 1) * PEER_HALF:(2 * h + 2) * PEER_HALF]
        sa = lax.dot_general(keys_ref[0], qa, NT_DIMS, preferred_element_type=F32)
        sb = lax.dot_general(keys_ref[1], qb, NT_DIMS, preferred_element_type=F32)
        va = _top_values(sa, PEER_TOPK)
        vb = _top_values(sb, PEER_TOPK)
        vb_all = jnp.concatenate(vb, axis=0)
        cand = jnp.concatenate([va[a] + vb_all for a in range(PEER_TOPK)], axis=0)
        top = _top_values(cand, PEER_TOPK)
        tau = top[PEER_TOPK - 1]
        z = top[0] * 0.0
        for tv in top:
            z = z + jnp.exp(tv - top[0])
        sel = cand >= tau
        cut = jnp.full(sa.shape, inf, F32)
        for a in range(PEER_TOPK):
            sel_a = sel[a * PEER_TOPK:(a + 1) * PEER_TOPK]
            cut_a = jnp.min(jnp.where(sel_a, vb_all, inf), axis=0, keepdims=True)
            cut = jnp.minimum(cut, jnp.where(sa == va[a], cut_a, inf))
        sb_ref[h] = sb
        cut_ref[h] = cut
        ea_ref[h] = jnp.exp(sa - va[0])
        ebz_ref[h] = jnp.exp(sb - vb[0]) / z


def _peer_route(x1, wq_b, keys_b):
    seq, d_model = x1.shape
    tt = min(256, seq)
    out = jax.ShapeDtypeStruct((PEER_HEADS, PEER_NKEYS, seq), F32)
    ospec = pl.BlockSpec((PEER_HEADS, PEER_NKEYS, tt), lambda i: (0, 0, i))
    return pl.pallas_call(
        _peer_route_kernel,
        grid=(seq // tt,),
        in_specs=[pl.BlockSpec((tt, d_model), lambda i: (i, 0)),
                  pl.BlockSpec(wq_b.shape, lambda i: (0, 0)),
                  pl.BlockSpec(keys_b.shape, lambda i: (0, 0, 0))],
        out_specs=[ospec] * 4,
        out_shape=[out] * 4,
        compiler_params=_params(1),
        name="peer_route",
    )(x1, wq_b, keys_b)


def _gelu_tanh(x):
    c = math.sqrt(2.0 / math.pi)
    return 0.5 * x * (1.0 + jnp.tanh(c * (x + 0.044715 * (x * x * x))))


def _peer_dense_kernel(xt_ref, u_ref, vt_ref, sb_ref, cut_ref, ea_ref, ebz_ref, x1_ref, g_ref, b_ref,
                       o_ref, acc_ref, w_ref, *, alpha):
    c = pl.program_id(1)
    groups = u_ref.shape[0] // PEER_NKEYS

    @pl.when(c == 0)
    def _():
        acc_ref[...] = jnp.zeros_like(acc_ref)

    ht = jnp.dot(u_ref[...], xt_ref[...], preferred_element_type=F32)
    for ii in range(groups):
        i = c * groups + ii
        gate = jnp.zeros((PEER_NKEYS, ht.shape[1]), F32)
        for h in range(PEER_HEADS):
            cut_row = cut_ref[h, pl.ds(i, 1), :]
            ea_row = ea_ref[h, pl.ds(i, 1), :]
            gate = gate + jnp.where(sb_ref[h] >= cut_row, ebz_ref[h], 0.0) * ea_row
        act = _gelu_tanh(ht[ii * PEER_NKEYS:(ii + 1) * PEER_NKEYS])
        w_ref[ii * PEER_NKEYS:(ii + 1) * PEER_NKEYS, :] = (gate * act).astype(w_ref.dtype)
    acc_ref[...] += jnp.dot(vt_ref[...], w_ref[...], preferred_element_type=F32)

    @pl.when(c == pl.num_programs(1) - 1)
    def _():
        y = acc_ref[...].T + alpha * x1_ref[...]
        o_ref[...] = _layer_norm(y, g_ref[...], b_ref[...])


def _peer_dense(x1t, u_b, vt_b, sb, cut, ea, ebz, x1, g, b, alpha):
    seq, d_model = x1.shape
    n_exp = u_b.shape[0]
    tt = min(512, seq)
    ec = 512
    once = pl.Buffered(1)
    gspec = pl.BlockSpec((PEER_HEADS, PEER_NKEYS, tt), lambda i, c: (0, 0, i), pipeline_mode=once)
    row = pl.BlockSpec((1, d_model), lambda i, c: (0, 0))
    return pl.pallas_call(
        functools.partial(_peer_dense_kernel, alpha=alpha),
        grid=(seq // tt, n_exp // ec),
        in_specs=[pl.BlockSpec((d_model, tt), lambda i, c: (0, i), pipeline_mode=once),
                  pl.BlockSpec((ec, d_model), lambda i, c: (c, 0)),
                  pl.BlockSpec((d_model, ec), lambda i, c: (0, c)),
                  gspec, gspec, gspec, gspec,
                  pl.BlockSpec((tt, d_model), lambda i, c: (i, 0), pipeline_mode=once),
                  row, row],
        out_specs=pl.BlockSpec((tt, d_model), lambda i, c: (i, 0)),
        out_shape=jax.ShapeDtypeStruct((seq, d_model), F32),
        scratch_shapes=[pltpu.VMEM((d_model, tt), F32), pltpu.VMEM((ec, tt), BF16)],
        compiler_params=_params(2),
        name="peer_dense",
    )(x1t, u_b, vt_b, sb, cut, ea, ebz, x1, g, b)


def _ple_kernel(x_ref, p_ref, wg_ref, wp_ref, g_ref, b_ref, o_ref, *, alpha):
    x = x_ref[...]
    gate = jax.nn.sigmoid(jnp.dot(x.astype(BF16), wg_ref[...], preferred_element_type=F32))
    emb = jnp.dot(p_ref[...].astype(BF16), wp_ref[...], preferred_element_type=F32)
    o_ref[...] = _layer_norm(alpha * x + gate * emb, g_ref[...], b_ref[...])


def _ple(x2, p2d, wg_b, wp_b, g, b, alpha):
    seq, d_model = x2.shape
    tm = min(512, seq)
    row = pl.BlockSpec((1, d_model), lambda i: (0, 0))
    return pl.pallas_call(
        functools.partial(_ple_kernel, alpha=alpha),
        grid=(seq // tm,),
        in_specs=[pl.BlockSpec((tm, d_model), lambda i: (i, 0)),
                  pl.BlockSpec((tm, p2d.shape[1]), lambda i: (i, 0)),
                  pl.BlockSpec(wg_b.shape, lambda i: (0, 0)),
                  pl.BlockSpec(wp_b.shape, lambda i: (0, 0)),
                  row, row],
        out_specs=pl.BlockSpec((tm, d_model), lambda i: (i, 0)),
        out_shape=jax.ShapeDtypeStruct((seq, d_model), F32),
        compiler_params=_params(1),
        name="ple",
    )(x2, p2d, wg_b, wp_b, g, b)


def kernel(x, p, w_in, w_out, lambda_q1, lambda_k1, lambda_q2, lambda_k2, diff_subln_g, ln_g, ln_b,
           peer_wq, peer_keys, peer_u, peer_v, ple_gate, ple_proj):
    batch, seq, d_model = x.shape
    assert batch == 1 and w_in.shape[0] == DEPTH
    alpha = (2.0 * DEPTH) ** 0.25
    x2d = x[0]
    for i in range(DEPTH):
        proj = _in_proj(x2d.astype(BF16), w_in[i].astype(BF16), seq)
        od = _diff_attn(proj, lambda_q1[i][None], lambda_k1[i][None], lambda_q2[i][None],
                        lambda_k2[i][None], diff_subln_g[i][None], seq, i)
        osb = _sb_attn(proj, seq)
        x1, x1t = _out_proj(od, osb, w_out[i].astype(BF16), x2d, ln_g[i, 0][None], ln_b[i, 0][None],
                            alpha)
        sb, cut, ea, ebz = _peer_route(x1, peer_wq[i].astype(BF16), peer_keys[i].astype(BF16))
        x2 = _peer_dense(x1t, peer_u[i].astype(BF16), peer_v[i].T.astype(BF16), sb, cut, ea, ebz, x1,
                         ln_g[i, 1][None], ln_b[i, 1][None], alpha)
        x2d = _ple(x2, p[i, 0], ple_gate[i].astype(BF16), ple_proj[i].astype(BF16),
                   ln_g[i, 2][None], ln_b[i, 2][None], alpha)
    return x2d[None]
```

```python
import functools
import math

import jax
import jax.numpy as jnp
import numpy as np
from jax import lax
from jax.experimental import pallas as pl
from jax.experimental.pallas import tpu as pltpu

F32 = jnp.float32
BF16 = jnp.bfloat16

LANES = 128
V7X_VMEM_BYTES = 64 * 1024 * 1024
VMEM_LIMIT = V7X_VMEM_BYTES * 3 // 4

DIFF_HEADS = 8
SB_HEADS = 8
HEAD_DIM = 128
DIFF_QK_DIM = HEAD_DIM // 2
ROT_DIM = DIFF_QK_DIM // 4
ROPE_THETA = 500000.0
PEER_HEADS = 8
PEER_NKEYS = 128
PEER_HALF = 128
PEER_TOPK = 16
LN_EPS = 1e-5
DEPTH = 1
ATTN_Q_TILE = 512
ATTN_K_TILE = 256
ATTN_GROUP = 2

NT_DIMS = (((1,), (1,)), ((), ()))


def _params(n_grid):
    return pltpu.CompilerParams(dimension_semantics=("arbitrary",) * n_grid,
                                vmem_limit_bytes=VMEM_LIMIT)


def _layer_norm(y, g, b):
    mu = jnp.mean(y, axis=-1, keepdims=True)
    yc = y - mu
    var = jnp.mean(yc * yc, axis=-1, keepdims=True)
    return yc * lax.rsqrt(var + LN_EPS) * g + b


def _in_proj_kernel(x_ref, w_ref, c_ref, s1_ref, s2_ref, o_ref, *, n_rope_tiles):
    j = pl.program_id(0)
    acc = jnp.dot(x_ref[...], w_ref[...], preferred_element_type=F32)

    @pl.when(j >= n_rope_tiles)
    def _():
        o_ref[...] = acc.astype(o_ref.dtype)

    @pl.when(j < n_rope_tiles)
    def _():
        c = c_ref[0]
        s1 = s1_ref[0]
        s2 = s2_ref[0]
        for hb in range(acc.shape[1] // LANES):
            t = acc[:, hb * LANES:(hb + 1) * LANES]
            half = ROT_DIM // 2
            r = (t * c + pltpu.roll(t, half, axis=1) * s1
                 + pltpu.roll(t, LANES - half, axis=1) * s2)
            o_ref[:, hb * LANES:(hb + 1) * LANES] = r.astype(o_ref.dtype)


def _rope_lane_tables(seq):
    half = ROT_DIM // 2
    lane = np.arange(LANES) % DIFF_QK_DIM
    holds_a = lane < half
    holds_b = (lane >= half) & (lane < ROT_DIM)
    pos = jnp.arange(seq, dtype=F32)
    inv = ROPE_THETA ** (-jnp.arange(0, ROT_DIM, 2, dtype=F32) / ROT_DIM)
    ang = pos[:, None] * inv[lane % half][None, :]
    cos, sin = jnp.cos(ang), jnp.sin(ang)
    tabs = (jnp.where(holds_a | holds_b, cos, 1.0), jnp.where(holds_b, sin, 0.0),
            jnp.where(holds_a, -sin, 0.0))
    qscale = DIFF_QK_DIM ** -0.5
    return [jnp.stack([t * qscale, t]) for t in tabs]


def _in_proj(xb, w_in_b, seq):
    d_model = xb.shape[1]
    n_cols = w_in_b.shape[1]
    tn = DIFF_HEADS * HEAD_DIM
    tm = min(1024, seq)
    c_tab, s1_tab, s2_tab = _rope_lane_tables(seq)
    tab_spec = pl.BlockSpec((1, tm, LANES), lambda j, i: (jnp.minimum(j, 1), i, 0))
    return pl.pallas_call(
        functools.partial(_in_proj_kernel, n_rope_tiles=2),
        grid=(n_cols // tn, seq // tm),
        in_specs=[pl.BlockSpec((tm, d_model), lambda j, i: (i, 0)),
                  pl.BlockSpec((d_model, tn), lambda j, i: (0, j)),
                  tab_spec, tab_spec, tab_spec],
        out_specs=pl.BlockSpec((tm, tn), lambda j, i: (i, j)),
        out_shape=jax.ShapeDtypeStruct((seq, n_cols), BF16),
        compiler_params=_params(2),
        name="in_proj",
    )(xb, w_in_b, c_tab, s1_tab, s2_tab)


def _wavefront(n_streams, stages):
    vals = [None] * n_streams
    for t in range(n_streams + len(stages) - 1):
        for j in reversed(range(len(stages))):
            s = t - j
            if 0 <= s < n_streams:
                vals[s] = stages[j](s, vals[s])
    return tuple(vals)


def _diff_attn_kernel(q_ref, k_ref, v_ref, lq1_ref, lk1_ref, lq2_ref, lk2_ref, g_ref, o_ref,
                      m_ref, l_ref, acc_ref, *, tq, tk, lambda_init, group):
    qi = pl.program_id(1)
    ratio = tq // tk
    lane = lax.broadcasted_iota(jnp.int32, (tq, HEAD_DIM), 1)
    qs = []
    for g in range(group):
        q = q_ref[:, g * HEAD_DIM:(g + 1) * HEAD_DIM]
        zero = jnp.zeros_like(q)
        qs.append(jnp.where(lane < DIFF_QK_DIM, q, zero))
        qs.append(jnp.where(lane < DIFF_QK_DIM, zero, q))
    row = lax.broadcasted_iota(jnp.int32, (tq, tk), 0)
    col = lax.broadcasted_iota(jnp.int32, (tq, tk), 1)

    def chunks(ki, valid):
        start = pl.multiple_of(ki * tk, tk)

        def cols(s):
            return slice((s // 2) * HEAD_DIM, (s // 2 + 1) * HEAD_DIM)

        def qk(s, _):
            return lax.dot_general(qs[s], k_ref[pl.ds(start, tk), cols(s)], NT_DIMS,
                                   preferred_element_type=F32)

        def softmax(s, sc):
            if valid is not None:
                sc = jnp.where(valid, sc, -jnp.inf)
            m = m_ref[s]
            m_new = jnp.maximum(m, jnp.max(sc, axis=-1, keepdims=True))
            p = jnp.exp(sc - jnp.concatenate([m_new] * (tk // LANES), axis=1))
            a = jnp.exp(m - m_new)
            m_ref[s] = m_new
            l_ref[s] = a * l_ref[s] + jnp.sum(p, axis=-1, keepdims=True)
            return a, p.astype(BF16)

        def pv(s, x):
            a, p = x
            acc_ref[s] = a * acc_ref[s] + jnp.dot(p, v_ref[pl.ds(start, tk), cols(s)],
                                                  preferred_element_type=F32)

        _wavefront(2 * group, (qk, softmax, pv))

    m_ref[...] = jnp.full(m_ref.shape, -1e30, F32)
    l_ref[...] = jnp.zeros(l_ref.shape, F32)
    acc_ref[...] = jnp.zeros(acc_ref.shape, F32)
    for d in range(ratio):
        chunks(qi * ratio + d, col + d * tk <= row)

    def body(ki, carry):
        chunks(ki, None)
        return carry

    lax.fori_loop(0, qi * ratio, body, 0)

    lam = (jnp.exp(jnp.sum(lq1_ref[...] * lk1_ref[...], axis=-1, keepdims=True))
           - jnp.exp(jnp.sum(lq2_ref[...] * lk2_ref[...], axis=-1, keepdims=True))
           + lambda_init)
    for g in range(group):
        od = acc_ref[2 * g] / l_ref[2 * g] - lam * (acc_ref[2 * g + 1] / l_ref[2 * g + 1])
        od = od * lax.rsqrt(jnp.mean(od * od, axis=-1, keepdims=True) + LN_EPS)
        od = od * g_ref[...] * (1.0 - lambda_init)
        o_ref[:, g * HEAD_DIM:(g + 1) * HEAD_DIM] = od.astype(o_ref.dtype)


def _diff_attn(proj, lq1, lk1, lq2, lk2, g, seq, layer):
    tq = min(ATTN_Q_TILE, seq)
    tk = min(ATTN_K_TILE, seq)
    gw = ATTN_GROUP * HEAD_DIM
    n_groups = DIFF_HEADS // ATTN_GROUP
    lambda_init = 0.8 - 0.6 * math.exp(-0.3 * layer)
    vec = pl.BlockSpec((1, DIFF_QK_DIM), lambda h, i: (0, 0))
    once = pl.Buffered(1)
    return pl.pallas_call(
        functools.partial(_diff_attn_kernel, tq=tq, tk=tk, lambda_init=lambda_init,
                          group=ATTN_GROUP),
        grid=(n_groups, seq // tq),
        in_specs=[pl.BlockSpec((tq, gw), lambda h, i: (i, h)),
                  pl.BlockSpec((seq, gw), lambda h, i: (0, n_groups + h), pipeline_mode=once),
                  pl.BlockSpec((seq, gw), lambda h, i: (0, 2 * n_groups + h), pipeline_mode=once),
                  vec, vec, vec, vec,
                  pl.BlockSpec((1, HEAD_DIM), lambda h, i: (0, 0))],
        out_specs=pl.BlockSpec((tq, gw), lambda h, i: (i, h)),
        out_shape=jax.ShapeDtypeStruct((seq, DIFF_HEADS * HEAD_DIM), BF16),
        scratch_shapes=[pltpu.VMEM((2 * ATTN_GROUP, tq, LANES), F32),
                        pltpu.VMEM((2 * ATTN_GROUP, tq, LANES), F32),
                        pltpu.VMEM((2 * ATTN_GROUP, tq, HEAD_DIM), F32)],
        compiler_params=_params(2),
        name="diff_attn",
    )(proj, proj, proj, lq1, lk1, lq2, lk2, g)


def _sb_attn_kernel(q_ref, k_ref, v_ref, o_ref, drop_ref, acc_ref, *, tq, tk, scale, group):
    qi = pl.program_id(1)
    ratio = tq // tk
    tri = jnp.where(lax.broadcasted_iota(jnp.int32, (tk, tk), 0)
                    > lax.broadcasted_iota(jnp.int32, (tk, tk), 1), 1.0, 0.0).astype(BF16)
    tri2 = jnp.concatenate([tri, tri], axis=0)
    scale2 = scale * math.log2(math.e)
    row = lax.broadcasted_iota(jnp.int32, (tq, tk), 0)
    col = lax.broadcasted_iota(jnp.int32, (tq, tk), 1)

    def chunks(ki, valid):
        start = pl.multiple_of(ki * tk, tk)

        def cols(g):
            return slice(g * HEAD_DIM, (g + 1) * HEAD_DIM)

        def qk(g, _):
            return lax.dot_general(q_ref[:, cols(g)], k_ref[pl.ds(start, tk), cols(g)], NT_DIMS,
                                   preferred_element_type=F32) * scale2

        def gates(g, z):
            z_neg = jnp.minimum(z, 0.0)
            z_pos = jnp.maximum(z, 0.0)
            sp = jnp.log2(1.0 + jnp.exp2(z_neg - z_pos))
            log_beta = z_neg - sp
            skip = z_pos + sp
            if valid is not None:
                skip = jnp.where(valid, skip, 0.0)
            hi = skip.astype(BF16)
            lo = (skip - hi.astype(F32)).astype(BF16)
            return log_beta, jnp.sum(skip, axis=-1, keepdims=True), jnp.concatenate([hi, lo], axis=1)

        def cumsum(g, x):
            log_beta, total, split = x
            return log_beta, total, jnp.dot(split, tri2, preferred_element_type=F32)

        def weights(g, x):
            log_beta, total, suffix = x
            a = jnp.exp2(log_beta - suffix - jnp.concatenate([drop_ref[g]] * (tk // LANES), axis=1))
            if valid is not None:
                a = jnp.where(valid, a, 0.0)
            return total, a.astype(BF16)

        def pv(g, x):
            total, a = x
            acc_ref[g] += jnp.dot(a, v_ref[pl.ds(start, tk), cols(g)], preferred_element_type=F32)
            drop_ref[g] += total

        _wavefront(group, (qk, gates, cumsum, weights, pv))

    drop_ref[...] = jnp.zeros(drop_ref.shape, F32)
    acc_ref[...] = jnp.zeros(acc_ref.shape, F32)
    for d in reversed(range(ratio)):
        chunks(qi * ratio + d, col + d * tk < row)

    def body(step, carry):
        chunks(qi * ratio - 1 - step, None)
        return carry

    lax.fori_loop(0, qi * ratio, body, 0)
    for g in range(group):
        o_ref[:, g * HEAD_DIM:(g + 1) * HEAD_DIM] = acc_ref[g].astype(o_ref.dtype)


def _sb_attn(proj, seq):
    tq = min(ATTN_Q_TILE, seq)
    tk = min(ATTN_K_TILE, seq)
    gw = ATTN_GROUP * HEAD_DIM
    n_groups = SB_HEADS // ATTN_GROUP
    base = 3 * DIFF_HEADS // ATTN_GROUP
    once = pl.Buffered(1)
    return pl.pallas_call(
        functools.partial(_sb_attn_kernel, tq=tq, tk=tk, scale=HEAD_DIM ** -0.5, group=ATTN_GROUP),
        grid=(n_groups, seq // tq),
        in_specs=[pl.BlockSpec((tq, gw), lambda h, i: (i, base + h)),
                  pl.BlockSpec((seq, gw), lambda h, i: (0, base + n_groups + h), pipeline_mode=once),
                  pl.BlockSpec((seq, gw), lambda h, i: (0, base + 2 * n_groups + h),
                               pipeline_mode=once)],
        out_specs=pl.BlockSpec((tq, gw), lambda h, i: (i, h)),
        out_shape=jax.ShapeDtypeStruct((seq, SB_HEADS * HEAD_DIM), BF16),
        scratch_shapes=[pltpu.VMEM((ATTN_GROUP, tq, LANES), F32),
                        pltpu.VMEM((ATTN_GROUP, tq, HEAD_DIM), F32)],
        compiler_params=_params(2),
        name="sb_attn",
    )(proj, proj, proj)


def _out_proj_kernel(od_ref, osb_ref, w_ref, x_ref, g_ref, b_ref, o_ref, ot_ref, *, alpha):
    n_diff = od_ref.shape[1]
    y = jnp.dot(od_ref[...], w_ref[:n_diff, :], preferred_element_type=F32)
    y = y + jnp.dot(osb_ref[...], w_ref[n_diff:, :], preferred_element_type=F32)
    y = y + alpha * x_ref[...]
    out = _layer_norm(y, g_ref[...], b_ref[...])
    o_ref[...] = out
    ot_ref[...] = out.T.astype(ot_ref.dtype)


def _out_proj(od, osb, w_out_b, x2d, g, b, alpha):
    seq, d_model = x2d.shape
    tm = min(512, seq)
    row = pl.BlockSpec((1, d_model), lambda i: (0, 0))
    return pl.pallas_call(
        functools.partial(_out_proj_kernel, alpha=alpha),
        grid=(seq // tm,),
        in_specs=[pl.BlockSpec((tm, od.shape[1]), lambda i: (i, 0)),
                  pl.BlockSpec((tm, osb.shape[1]), lambda i: (i, 0)),
                  pl.BlockSpec(w_out_b.shape, lambda i: (0, 0)),
                  pl.BlockSpec((tm, d_model), lambda i: (i, 0)),
                  row, row],
        out_specs=[pl.BlockSpec((tm, d_model), lambda i: (i, 0)),
                   pl.BlockSpec((d_model, tm), lambda i: (0, i))],
        out_shape=[jax.ShapeDtypeStruct((seq, d_model), F32),
                   jax.ShapeDtypeStruct((d_model, seq), BF16)],
        compiler_params=_params(1),
        name="out_proj",
    )(od, osb, w_out_b, x2d, g, b)


def _top_values(s, k):
    out = []
    for _ in range(k):
        m = jnp.max(s, axis=0, keepdims=True)
        out.append(m)
        s = jnp.where(s == m, -jnp.inf, s)
    return out


def _peer_route_kernel(x_ref, wq_ref, keys_ref, sb_ref, cut_ref, ea_ref, ebz_ref):
    q = jnp.dot(x_ref[...].astype(BF16), wq_ref[...], preferred_element_type=F32).astype(BF16)
    inf = jnp.inf
    for h in range(PEER_HEADS):
        qa = q[:, (2 * h) * PEER_HALF:(2 * h + 1) * PEER_HALF]
        qb = q[:, (2 * h + 1) * PEER_HALF:(2 * h + 2) * PEER_HALF]
        sa = lax.dot_general(keys_ref[0], qa, NT_DIMS, preferred_element_type=F32)
        sb = lax.dot_general(keys_ref[1], qb, NT_DIMS, preferred_element_type=F32)
        va = _top_values(sa, PEER_TOPK)
        vb = _top_values(sb, PEER_TOPK)
        vb_all = jnp.concatenate(vb, axis=0)
        cand = jnp.concatenate([va[a] + vb_all for a in range(PEER_TOPK)], axis=0)
        top = _top_values(cand, PEER_TOPK)
        tau = top[PEER_TOPK - 1]
        z = top[0] * 0.0
        for tv in top:
            z = z + jnp.exp(tv - top[0])
        sel = cand >= tau
        cut = jnp.full(sa.shape, inf, F32)
        for a in range(PEER_TOPK):
            sel_a = sel[a * PEER_TOPK:(a + 1) * PEER_TOPK]
            cut_a = jnp.min(jnp.where(sel_a, vb_all, inf), axis=0, keepdims=True)
            cut = jnp.minimum(cut, jnp.where(sa == va[a], cut_a, inf))
        sb_ref[h] = sb
        cut_ref[h] = cut
        ea_ref[h] = jnp.exp(sa - va[0])
        ebz_ref[h] = jnp.exp(sb - vb[0]) / z


def _peer_route(x1, wq_b, keys_b):
    seq, d_model = x1.shape
    tt = min(256, seq)
    out = jax.ShapeDtypeStruct((PEER_HEADS, PEER_NKEYS, seq), F32)
    ospec = pl.BlockSpec((PEER_HEADS, PEER_NKEYS, tt), lambda i: (0, 0, i))
    return pl.pallas_call(
        _peer_route_kernel,
        grid=(seq // tt,),
        in_specs=[pl.BlockSpec((tt, d_model), lambda i: (i, 0)),
                  pl.BlockSpec(wq_b.shape, lambda i: (0, 0)),
                  pl.BlockSpec(keys_b.shape, lambda i: (0, 0, 0))],
        out_specs=[ospec] * 4,
        out_shape=[out] * 4,
        compiler_params=_params(1),
        name="peer_route",
    )(x1, wq_b, keys_b)


def _gelu_tanh(x):
    c = math.sqrt(2.0 / math.pi)
    return 0.5 * x * (1.0 + jnp.tanh(c * (x + 0.044715 * (x * x * x))))


def _peer_dense_kernel(xt_ref, u_ref, vt_ref, sb_ref, cut_ref, ea_ref, ebz_ref, x1_ref, g_ref, b_ref,
                       o_ref, acc_ref, w_ref, *, alpha):
    c = pl.program_id(1)
    groups = u_ref.shape[0] // PEER_NKEYS

    @pl.when(c == 0)
    def _():
        acc_ref[...] = jnp.zeros_like(acc_ref)

    ht = jnp.dot(u_ref[...], xt_ref[...], preferred_element_type=F32)
    for ii in range(groups):
        i = c * groups + ii
        gate = jnp.zeros((PEER_NKEYS, ht.shape[1]), F32)
        for h in range(PEER_HEADS):
            cut_row = cut_ref[h, pl.ds(i, 1), :]
            ea_row = ea_ref[h, pl.ds(i, 1), :]
            gate = gate + jnp.where(sb_ref[h] >= cut_row, ebz_ref[h], 0.0) * ea_row
        act = _gelu_tanh(ht[ii * PEER_NKEYS:(ii + 1) * PEER_NKEYS])
        w_ref[ii * PEER_NKEYS:(ii + 1) * PEER_NKEYS, :] = (gate * act).astype(w_ref.dtype)
    acc_ref[...] += jnp.dot(vt_ref[...], w_ref[...], preferred_element_type=F32)

    @pl.when(c == pl.num_programs(1) - 1)
    def _():
        y = acc_ref[...].T + alpha * x1_ref[...]
        o_ref[...] = _layer_norm(y, g_ref[...], b_ref[...])


def _peer_dense(x1t, u_b, vt_b, sb, cut, ea, ebz, x1, g, b, alpha):
    seq, d_model = x1.shape
    n_exp = u_b.shape[0]
    tt = min(512, seq)
    ec = 512
    once = pl.Buffered(1)
    gspec = pl.BlockSpec((PEER_HEADS, PEER_NKEYS, tt), lambda i, c: (0, 0, i), pipeline_mode=once)
    row = pl.BlockSpec((1, d_model), lambda i, c: (0, 0))
    return pl.pallas_call(
        functools.partial(_peer_dense_kernel, alpha=alpha),
        grid=(seq // tt, n_exp // ec),
        in_specs=[pl.BlockSpec((d_model, tt), lambda i, c: (0, i), pipeline_mode=once),
                  pl.BlockSpec((ec, d_model), lambda i, c: (c, 0)),
                  pl.BlockSpec((d_model, ec), lambda i, c: (0, c)),
                  gspec, gspec, gspec, gspec,
                  pl.BlockSpec((tt, d_model), lambda i, c: (i, 0), pipeline_mode=once),
                  row, row],
        out_specs=pl.BlockSpec((tt, d_model), lambda i, c: (i, 0)),
        out_shape=jax.ShapeDtypeStruct((seq, d_model), F32),
        scratch_shapes=[pltpu.VMEM((d_model, tt), F32), pltpu.VMEM((ec, tt), BF16)],
        compiler_params=_params(2),
        name="peer_dense",
    )(x1t, u_b, vt_b, sb, cut, ea, ebz, x1, g, b)


def _ple_kernel(x_ref, p_ref, wg_ref, wp_ref, g_ref, b_ref, o_ref, *, alpha):
    x = x_ref[...]
    gate = jax.nn.sigmoid(jnp.dot(x.astype(BF16), wg_ref[...], preferred_element_type=F32))
    emb = jnp.dot(p_ref[...].astype(BF16), wp_ref[...], preferred_element_type=F32)
    o_ref[...] = _layer_norm(alpha * x + gate * emb, g_ref[...], b_ref[...])


def _ple(x2, p2d, wg_b, wp_b, g, b, alpha):
    seq, d_model = x2.shape
    tm = min(512, seq)
    row = pl.BlockSpec((1, d_model), lambda i: (0, 0))
    return pl.pallas_call(
        functools.partial(_ple_kernel, alpha=alpha),
        grid=(seq // tm,),
        in_specs=[pl.BlockSpec((tm, d_model), lambda i: (i, 0)),
                  pl.BlockSpec((tm, p2d.shape[1]), lambda i: (i, 0)),
                  pl.BlockSpec(wg_b.shape, lambda i: (0, 0)),
                  pl.BlockSpec(wp_b.shape, lambda i: (0, 0)),
                  row, row],
        out_specs=pl.BlockSpec((tm, d_model), lambda i: (i, 0)),
        out_shape=jax.ShapeDtypeStruct((seq, d_model), F32),
        compiler_params=_params(1),
        name="ple",
    )(x2, p2d, wg_b, wp_b, g, b)


def kernel(x, p, w_in, w_out, lambda_q1, lambda_k1, lambda_q2, lambda_k2, diff_subln_g, ln_g, ln_b,
           peer_wq, peer_keys, peer_u, peer_v, ple_gate, ple_proj):
    batch, seq, d_model = x.shape
    assert batch == 1 and w_in.shape[0] == DEPTH
    alpha = (2.0 * DEPTH) ** 0.25
    x2d = x[0]
    for i in range(DEPTH):
        proj = _in_proj(x2d.astype(BF16), w_in[i].astype(BF16), seq)
        od = _diff_attn(proj, lambda_q1[i][None], lambda_k1[i][None], lambda_q2[i][None],
                        lambda_k2[i][None], diff_subln_g[i][None], seq, i)
        osb = _sb_attn(proj, seq)
        x1, x1t = _out_proj(od, osb, w_out[i].astype(BF16), x2d, ln_g[i, 0][None], ln_b[i, 0][None],
                            alpha)
        sb, cut, ea, ebz = _peer_route(x1, peer_wq[i].astype(BF16), peer_keys[i].astype(BF16))
        x2 = _peer_dense(x1t, peer_u[i].astype(BF16), peer_v[i].T.astype(BF16), sb, cut, ea, ebz, x1,
                         ln_g[i, 1][None], ln_b[i, 1][None], alpha)
        x2d = _ple(x2, p[i, 0], ple_gate[i].astype(BF16), ple_proj[i].astype(BF16),
                   ln_g[i, 2][None], ln_b[i, 2][None], alpha)
    return x2d[None]
```

```python
import functools
import math

import jax
import jax.numpy as jnp
import numpy as np
from jax import lax
from jax.experimental import pallas as pl
from jax.experimental.pallas import tpu as pltpu

F32 = jnp.float32
BF16 = jnp.bfloat16

LANES = 128
V7X_VMEM_BYTES = 64 * 1024 * 1024
VMEM_LIMIT = V7X_VMEM_BYTES * 7 // 8

DIFF_HEADS = 8
SB_HEADS = 8
HEAD_DIM = 128
DIFF_QK_DIM = HEAD_DIM // 2
ROT_DIM = DIFF_QK_DIM // 4
ROPE_THETA = 500000.0
PEER_HEADS = 8
PEER_NKEYS = 128
PEER_HALF = 128
PEER_TOPK = 16
LN_EPS = 1e-5
DEPTH = 1
ATTN_Q_TILE = 512
ATTN_K_TILE = 256
ATTN_GROUP = 2

NT_DIMS = (((1,), (1,)), ((), ()))


def _params(n_grid):
    return pltpu.CompilerParams(dimension_semantics=("arbitrary",) * n_grid,
                                vmem_limit_bytes=VMEM_LIMIT)


def _layer_norm(y, g, b):
    mu = jnp.mean(y, axis=-1, keepdims=True)
    yc = y - mu
    var = jnp.mean(yc * yc, axis=-1, keepdims=True)
    return yc * lax.rsqrt(var + LN_EPS) * g + b


def _in_proj_kernel(x_ref, w_ref, c_ref, s1_ref, s2_ref, o_ref, *, n_rope_tiles):
    j = pl.program_id(0)
    acc = jnp.dot(x_ref[...], w_ref[...], preferred_element_type=F32)

    @pl.when(j >= n_rope_tiles)
    def _():
        o_ref[...] = acc.astype(o_ref.dtype)

    @pl.when(j < n_rope_tiles)
    def _():
        c = c_ref[0]
        s1 = s1_ref[0]
        s2 = s2_ref[0]
        for hb in range(acc.shape[1] // LANES):
            t = acc[:, hb * LANES:(hb + 1) * LANES]
            half = ROT_DIM // 2
            r = (t * c + pltpu.roll(t, half, axis=1) * s1
                 + pltpu.roll(t, LANES - half, axis=1) * s2)
            o_ref[:, hb * LANES:(hb + 1) * LANES] = r.astype(o_ref.dtype)


def _rope_lane_tables(seq):
    half = ROT_DIM // 2
    lane = np.arange(LANES) % DIFF_QK_DIM
    holds_a = lane < half
    holds_b = (lane >= half) & (lane < ROT_DIM)
    pos = jnp.arange(seq, dtype=F32)
    inv = ROPE_THETA ** (-jnp.arange(0, ROT_DIM, 2, dtype=F32) / ROT_DIM)
    ang = pos[:, None] * inv[lane % half][None, :]
    cos, sin = jnp.cos(ang), jnp.sin(ang)
    tabs = (jnp.where(holds_a | holds_b, cos, 1.0), jnp.where(holds_b, sin, 0.0),
            jnp.where(holds_a, -sin, 0.0))
    qscale = DIFF_QK_DIM ** -0.5
    return [jnp.stack([t * qscale, t]) for t in tabs]


def _in_proj(xb, w_in_b, seq):
    d_model = xb.shape[1]
    n_cols = w_in_b.shape[1]
    tn = DIFF_HEADS * HEAD_DIM
    tm = min(1024, seq)
    c_tab, s1_tab, s2_tab = _rope_lane_tables(seq)
    tab_spec = pl.BlockSpec((1, tm, LANES), lambda j, i: (jnp.minimum(j, 1), i, 0))
    return pl.pallas_call(
        functools.partial(_in_proj_kernel, n_rope_tiles=2),
        grid=(n_cols // tn, seq // tm),
        in_specs=[pl.BlockSpec((tm, d_model), lambda j, i: (i, 0)),
                  pl.BlockSpec((d_model, tn), lambda j, i: (0, j)),
                  tab_spec, tab_spec, tab_spec],
        out_specs=pl.BlockSpec((tm, tn), lambda j, i: (i, j)),
        out_shape=jax.ShapeDtypeStruct((seq, n_cols), BF16),
        compiler_params=_params(2),
        name="in_proj",
    )(xb, w_in_b, c_tab, s1_tab, s2_tab)


def _wavefront(n_streams, stages):
    vals = [None] * n_streams
    for t in range(n_streams + len(stages) - 1):
        for j in reversed(range(len(stages))):
            s = t - j
            if 0 <= s < n_streams:
                vals[s] = stages[j](s, vals[s])
    return tuple(vals)


def _diff_attn_kernel(q_ref, k_ref, v_ref, lq1_ref, lk1_ref, lq2_ref, lk2_ref, g_ref, o_ref,
                      m_ref, l_ref, acc_ref, *, tq, tk, lambda_init, group):
    qi = pl.program_id(1)
    ratio = tq // tk
    lane = lax.broadcasted_iota(jnp.int32, (tq, HEAD_DIM), 1)
    qs = []
    for g in range(group):
        q = q_ref[:, g * HEAD_DIM:(g + 1) * HEAD_DIM]
        zero = jnp.zeros_like(q)
        qs.append(jnp.where(lane < DIFF_QK_DIM, q, zero))
        qs.append(jnp.where(lane < DIFF_QK_DIM, zero, q))
    row = lax.broadcasted_iota(jnp.int32, (tq, tk), 0)
    col = lax.broadcasted_iota(jnp.int32, (tq, tk), 1)

    def chunks(ki, valid):
        start = pl.multiple_of(ki * tk, tk)

        def cols(s):
            return slice((s // 2) * HEAD_DIM, (s // 2 + 1) * HEAD_DIM)

        def qk(s, _):
            return lax.dot_general(qs[s], k_ref[pl.ds(start, tk), cols(s)], NT_DIMS,
                                   preferred_element_type=F32)

        def softmax(s, sc):
            if valid is not None:
                sc = jnp.where(valid, sc, -jnp.inf)
            m = m_ref[s]
            m_new = jnp.maximum(m, jnp.max(sc, axis=-1, keepdims=True))
            p = jnp.exp(sc - jnp.concatenate([m_new] * (tk // LANES), axis=1))
            a = jnp.exp(m - m_new)
            m_ref[s] = m_new
            l_ref[s] = a * l_ref[s] + jnp.sum(p, axis=-1, keepdims=True)
            return a, p.astype(BF16)

        def pv(s, x):
            a, p = x
            acc_ref[s] = a * acc_ref[s] + jnp.dot(p, v_ref[pl.ds(start, tk), cols(s)],
                                                  preferred_element_type=F32)

        _wavefront(2 * group, (qk, softmax, pv))

    m_ref[...] = jnp.full(m_ref.shape, -1e30, F32)
    l_ref[...] = jnp.zeros(l_ref.shape, F32)
    acc_ref[...] = jnp.zeros(acc_ref.shape, F32)
    for d in range(ratio):
        chunks(qi * ratio + d, col + d * tk <= row)

    def body(ki, carry):
        chunks(ki, None)
        return carry

    lax.fori_loop(0, qi * ratio, body, 0)

    lam = (jnp.exp(jnp.sum(lq1_ref[...] * lk1_ref[...], axis=-1, keepdims=True))
           - jnp.exp(jnp.sum(lq2_ref[...] * lk2_ref[...], axis=-1, keepdims=True))
           + lambda_init)
    for g in range(group):
        od = acc_ref[2 * g] / l_ref[2 * g] - lam * (acc_ref[2 * g + 1] / l_ref[2 * g + 1])
        od = od * lax.rsqrt(jnp.mean(od * od, axis=-1, keepdims=True) + LN_EPS)
        od = od * g_ref[...] * (1.0 - lambda_init)
        o_ref[:, g * HEAD_DIM:(g + 1) * HEAD_DIM] = od.astype(o_ref.dtype)


def _diff_attn(proj, lq1, lk1, lq2, lk2, g, seq, layer):
    tq = min(ATTN_Q_TILE, seq)
    tk = min(ATTN_K_TILE, seq)
    gw = ATTN_GROUP * HEAD_DIM
    n_groups = DIFF_HEADS // ATTN_GROUP
    lambda_init = 0.8 - 0.6 * math.exp(-0.3 * layer)
    vec = pl.BlockSpec((1, DIFF_QK_DIM), lambda h, i: (0, 0))
    once = pl.Buffered(1)
    return pl.pallas_call(
        functools.partial(_diff_attn_kernel, tq=tq, tk=tk, lambda_init=lambda_init,
                          group=ATTN_GROUP),
        grid=(n_groups, seq // tq),
        in_specs=[pl.BlockSpec((tq, gw), lambda h, i: (i, h)),
                  pl.BlockSpec((seq, gw), lambda h, i: (0, n_groups + h), pipeline_mode=once),
                  pl.BlockSpec((seq, gw), lambda h, i: (0, 2 * n_groups + h), pipeline_mode=once),
                  vec, vec, vec, vec,
                  pl.BlockSpec((1, HEAD_DIM), lambda h, i: (0, 0))],
        out_specs=pl.BlockSpec((tq, gw), lambda h, i: (i, h)),
        out_shape=jax.ShapeDtypeStruct((seq, DIFF_HEADS * HEAD_DIM), BF16),
        scratch_shapes=[pltpu.VMEM((2 * ATTN_GROUP, tq, LANES), F32),
                        pltpu.VMEM((2 * ATTN_GROUP, tq, LANES), F32),
                        pltpu.VMEM((2 * ATTN_GROUP, tq, HEAD_DIM), F32)],
        compiler_params=_params(2),
        name="diff_attn",
    )(proj, proj, proj, lq1, lk1, lq2, lk2, g)


def _sb_attn_kernel(q_ref, k_ref, v_ref, o_ref, drop_ref, acc_ref, *, tq, tk, scale, group):
    qi = pl.program_id(1)
    ratio = tq // tk
    tri = jnp.where(lax.broadcasted_iota(jnp.int32, (tk, tk), 0)
                    > lax.broadcasted_iota(jnp.int32, (tk, tk), 1), 1.0, 0.0).astype(BF16)
    tri2 = jnp.concatenate([tri, tri], axis=0)
    scale2 = scale * math.log2(math.e)
    row = lax.broadcasted_iota(jnp.int32, (tq, tk), 0)
    col = lax.broadcasted_iota(jnp.int32, (tq, tk), 1)

    def chunks(ki, valid):
        start = pl.multiple_of(ki * tk, tk)

        def cols(g):
            return slice(g * HEAD_DIM, (g + 1) * HEAD_DIM)

        def qk(g, _):
            return lax.dot_general(q_ref[:, cols(g)], k_ref[pl.ds(start, tk), cols(g)], NT_DIMS,
                                   preferred_element_type=F32) * scale2

        def gates(g, z):
            z_neg = jnp.minimum(z, 0.0)
            z_pos = jnp.maximum(z, 0.0)
            sp = jnp.log2(1.0 + jnp.exp2(z_neg - z_pos))
            log_beta = z_neg - sp
            skip = z_pos + sp
            if valid is not None:
                skip = jnp.where(valid, skip, 0.0)
            hi = skip.astype(BF16)
            lo = (skip - hi.astype(F32)).astype(BF16)
            return log_beta, jnp.sum(skip, axis=-1, keepdims=True), jnp.concatenate([hi, lo], axis=1)

        def cumsum(g, x):
            log_beta, total, split = x
            return log_beta, total, jnp.dot(split, tri2, preferred_element_type=F32)

        def weights(g, x):
            log_beta, total, suffix = x
            a = jnp.exp2(log_beta - suffix - jnp.concatenate([drop_ref[g]] * (tk // LANES), axis=1))
            if valid is not None:
                a = jnp.where(valid, a, 0.0)
            return total, a.astype(BF16)

        def pv(g, x):
            total, a = x
            acc_ref[g] += jnp.dot(a, v_ref[pl.ds(start, tk), cols(g)], preferred_element_type=F32)
            drop_ref[g] += total

        _wavefront(group, (qk, gates, cumsum, weights, pv))

    drop_ref[...] = jnp.zeros(drop_ref.shape, F32)
    acc_ref[...] = jnp.zeros(acc_ref.shape, F32)
    for d in reversed(range(ratio)):
        chunks(qi * ratio + d, col + d * tk < row)

    def body(step, carry):
        chunks(qi * ratio - 1 - step, None)
        return carry

    lax.fori_loop(0, qi * ratio, body, 0)
    for g in range(group):
        o_ref[:, g * HEAD_DIM:(g + 1) * HEAD_DIM] = acc_ref[g].astype(o_ref.dtype)


def _sb_attn(proj, seq):
    tq = min(ATTN_Q_TILE, seq)
    tk = min(ATTN_K_TILE, seq)
    gw = ATTN_GROUP * HEAD_DIM
    n_groups = SB_HEADS // ATTN_GROUP
    base = 3 * DIFF_HEADS // ATTN_GROUP
    once = pl.Buffered(1)
    return pl.pallas_call(
        functools.partial(_sb_attn_kernel, tq=tq, tk=tk, scale=HEAD_DIM ** -0.5, group=ATTN_GROUP),
        grid=(n_groups, seq // tq),
        in_specs=[pl.BlockSpec((tq, gw), lambda h, i: (i, base + h)),
                  pl.BlockSpec((seq, gw), lambda h, i: (0, base + n_groups + h), pipeline_mode=once),
                  pl.BlockSpec((seq, gw), lambda h, i: (0, base + 2 * n_groups + h),
                               pipeline_mode=once)],
        out_specs=pl.BlockSpec((tq, gw), lambda h, i: (i, h)),
        out_shape=jax.ShapeDtypeStruct((seq, SB_HEADS * HEAD_DIM), BF16),
        scratch_shapes=[pltpu.VMEM((ATTN_GROUP, tq, LANES), F32),
                        pltpu.VMEM((ATTN_GROUP, tq, HEAD_DIM), F32)],
        compiler_params=_params(2),
        name="sb_attn",
    )(proj, proj, proj)


def _out_proj_kernel(od_ref, osb_ref, w_ref, x_ref, g_ref, b_ref, o_ref, ot_ref, *, alpha):
    n_diff = od_ref.shape[1]
    y = jnp.dot(od_ref[...], w_ref[:n_diff, :], preferred_element_type=F32)
    y = y + jnp.dot(osb_ref[...], w_ref[n_diff:, :], preferred_element_type=F32)
    y = y + alpha * x_ref[...]
    out = _layer_norm(y, g_ref[...], b_ref[...])
    o_ref[...] = out
    ot_ref[...] = out.T.astype(ot_ref.dtype)


def _out_proj(od, osb, w_out_b, x2d, g, b, alpha):
    seq, d_model = x2d.shape
    tm = min(512, seq)
    row = pl.BlockSpec((1, d_model), lambda i: (0, 0))
    return pl.pallas_call(
        functools.partial(_out_proj_kernel, alpha=alpha),
        grid=(seq // tm,),
        in_specs=[pl.BlockSpec((tm, od.shape[1]), lambda i: (i, 0)),
                  pl.BlockSpec((tm, osb.shape[1]), lambda i: (i, 0)),
                  pl.BlockSpec(w_out_b.shape, lambda i: (0, 0)),
                  pl.BlockSpec((tm, d_model), lambda i: (i, 0)),
                  row, row],
        out_specs=[pl.BlockSpec((tm, d_model), lambda i: (i, 0)),
                   pl.BlockSpec((d_model, tm), lambda i: (0, i))],
        out_shape=[jax.ShapeDtypeStruct((seq, d_model), F32),
                   jax.ShapeDtypeStruct((d_model, seq), BF16)],
        compiler_params=_params(1),
        name="out_proj",
    )(od, osb, w_out_b, x2d, g, b)


def _top_values(s, k):
    out = []
    for _ in range(k):
        m = jnp.max(s, axis=0, keepdims=True)
        out.append(m)
        s = jnp.where(s == m, -jnp.inf, s)
    return out


def _peer_route_kernel(x_ref, wq_ref, keys_ref, rank_ref, cnt_ref, ea_ref, ebz_ref):
    q = jnp.dot(x_ref[...].astype(BF16), wq_ref[...], preferred_element_type=F32).astype(BF16)
    half = PEER_TOPK // 2
    for h in range(PEER_HEADS):
        qa = q[:, (2 * h) * PEER_HALF:(2 * h + 1) * PEER_HALF]
        qb = q[:, (2 * h + 1) * PEER_HALF:(2 * h + 2) * PEER_HALF]
        sa = lax.dot_general(keys_ref[0], qa, NT_DIMS, preferred_element_type=F32)
        sb = lax.dot_general(keys_ref[1], qb, NT_DIMS, preferred_element_type=F32)
        va = _top_values(sa, PEER_TOPK)
        vb = _top_values(sb, PEER_TOPK)
        va_all = jnp.concatenate(va, axis=0)
        vb_all = jnp.concatenate(vb, axis=0)
        cand = jnp.concatenate([va[0] + vb_all]
                               + [va[a] + vb_all[:half] for a in range(1, half)]
                               + [va_all[half:] + vb[0]], axis=0)
        top = _top_values(cand, PEER_TOPK)
        tau = top[PEER_TOPK - 1]
        z = top[0] * 0.0
        for tv in top:
            z = z + jnp.exp(tv - top[0])
        sel = jnp.where(cand >= tau, 1.0, 0.0)
        cnt = jnp.zeros(sa.shape, F32)
        rank = jnp.zeros(sb.shape, F32)
        for a in range(PEER_TOPK):
            if a == 0:
                cnt_a = jnp.sum(sel[:PEER_TOPK], axis=0, keepdims=True)
            elif a < half:
                cnt_a = jnp.sum(sel[PEER_TOPK + (a - 1) * half:PEER_TOPK + a * half], axis=0,
                                keepdims=True)
            else:
                tail = PEER_TOPK + (half - 1) * half
                cnt_a = sel[tail + a - half:tail + a - half + 1]
            cnt = cnt + jnp.where(sa == va[a], cnt_a, 0.0)
            rank = rank + jnp.where(sb < vb[a], 1.0, 0.0)
        rank_ref[h] = rank.astype(rank_ref.dtype)
        cnt_ref[h] = cnt
        ea_ref[h] = jnp.exp(sa - va[0])
        ebz_ref[h] = (jnp.exp(sb - vb[0]) / z).astype(ebz_ref.dtype)


def _peer_route(x1, wq_b, keys_b):
    seq, d_model = x1.shape
    tt = min(256, seq)
    shape = (PEER_HEADS, PEER_NKEYS, seq)
    ospec = pl.BlockSpec((PEER_HEADS, PEER_NKEYS, tt), lambda i: (0, 0, i))
    return pl.pallas_call(
        _peer_route_kernel,
        grid=(seq // tt,),
        in_specs=[pl.BlockSpec((tt, d_model), lambda i: (i, 0)),
                  pl.BlockSpec(wq_b.shape, lambda i: (0, 0)),
                  pl.BlockSpec(keys_b.shape, lambda i: (0, 0, 0))],
        out_specs=[ospec] * 4,
        out_shape=[jax.ShapeDtypeStruct(shape, BF16), jax.ShapeDtypeStruct(shape, F32),
                   jax.ShapeDtypeStruct(shape, F32), jax.ShapeDtypeStruct(shape, BF16)],
        compiler_params=_params(1),
        name="peer_route",
    )(x1, wq_b, keys_b)


def _gelu_tanh(x):
    c = math.sqrt(2.0 / math.pi)
    return 0.5 * x * (1.0 + jnp.tanh(c * (x + 0.044715 * (x * x * x))))


def _bf16_rows(row, n):
    packed = jnp.broadcast_to(row, (16, row.shape[1])).astype(BF16)
    return jnp.concatenate([packed] * (n // 16), axis=0)


def _peer_dense_kernel(xt_ref, u_ref, vt_ref, rank_ref, cnt_ref, ea_ref, ebz_ref, x1_ref, g_ref, b_ref,
                       o_ref, acc_ref, *, alpha, sub):
    c = pl.program_id(1)
    n_sub = u_ref.shape[0] // sub
    groups = sub // PEER_NKEYS

    @pl.when(c == 0)
    def _():
        acc_ref[...] = jnp.zeros_like(acc_ref)

    def hidden(k, _):
        return jnp.dot(u_ref[k * sub:(k + 1) * sub, :], xt_ref[...], preferred_element_type=F32)

    def weights(k, ht):
        out = []
        for ii in range(groups):
            i = (c * n_sub + k) * groups + ii
            gate = jnp.zeros((PEER_NKEYS, ht.shape[1]), BF16)
            for h in range(PEER_HEADS):
                cnt_row = _bf16_rows(cnt_ref[h, pl.ds(i, 1), :], PEER_NKEYS)
                ea_row = _bf16_rows(ea_ref[h, pl.ds(i, 1), :], PEER_NKEYS)
                gate = gate + jnp.where(rank_ref[h] < cnt_row, ebz_ref[h], jnp.zeros_like(gate)) * ea_row
            act = _gelu_tanh(ht[ii * PEER_NKEYS:(ii + 1) * PEER_NKEYS])
            out.append(gate * act.astype(BF16))
        return jnp.concatenate(out, axis=0)

    def project(k, w):
        acc_ref[...] += jnp.dot(vt_ref[:, k * sub:(k + 1) * sub], w, preferred_element_type=F32)

    _wavefront(n_sub, (hidden, weights, project))

    @pl.when(c == pl.num_programs(1) - 1)
    def _():
        y = acc_ref[...].T + alpha * x1_ref[...]
        o_ref[...] = _layer_norm(y, g_ref[...], b_ref[...])


def _peer_dense(x1t, u_b, vt_b, rank, cnt, ea, ebz, x1, g, b, alpha):
    seq, d_model = x1.shape
    n_exp = u_b.shape[0]
    tt = min(512, seq)
    ec = 1024
    once = pl.Buffered(1)
    gspec = pl.BlockSpec((PEER_HEADS, PEER_NKEYS, tt), lambda i, c: (0, 0, i), pipeline_mode=once)
    row = pl.BlockSpec((1, d_model), lambda i, c: (0, 0))
    return pl.pallas_call(
        functools.partial(_peer_dense_kernel, alpha=alpha, sub=ec // 2),
        grid=(seq // tt, n_exp // ec),
        in_specs=[pl.BlockSpec((d_model, tt), lambda i, c: (0, i), pipeline_mode=once),
                  pl.BlockSpec((ec, d_model), lambda i, c: (c, 0)),
                  pl.BlockSpec((d_model, ec), lambda i, c: (0, c)),
                  gspec, gspec, gspec, gspec,
                  pl.BlockSpec((tt, d_model), lambda i, c: (i, 0), pipeline_mode=once),
                  row, row],
        out_specs=pl.BlockSpec((tt, d_model), lambda i, c: (i, 0)),
        out_shape=jax.ShapeDtypeStruct((seq, d_model), F32),
        scratch_shapes=[pltpu.VMEM((d_model, tt), F32)],
        compiler_params=_params(2),
        name="peer_dense",
    )(x1t, u_b, vt_b, rank, cnt, ea, ebz, x1, g, b)


def _ple_kernel(x_ref, p_ref, wg_ref, wp_ref, g_ref, b_ref, o_ref, *, alpha):
    x = x_ref[...]
    gate = jax.nn.sigmoid(jnp.dot(x.astype(BF16), wg_ref[...], preferred_element_type=F32))
    emb = jnp.dot(p_ref[...].astype(BF16), wp_ref[...], preferred_element_type=F32)
    o_ref[...] = _layer_norm(alpha * x + gate * emb, g_ref[...], b_ref[...])


def _ple(x2, p2d, wg_b, wp_b, g, b, alpha):
    seq, d_model = x2.shape
    tm = min(512, seq)
    row = pl.BlockSpec((1, d_model), lambda i: (0, 0))
    return pl.pallas_call(
        functools.partial(_ple_kernel, alpha=alpha),
        grid=(seq // tm,),
        in_specs=[pl.BlockSpec((tm, d_model), lambda i: (i, 0)),
                  pl.BlockSpec((tm, p2d.shape[1]), lambda i: (i, 0)),
                  pl.BlockSpec(wg_b.shape, lambda i: (0, 0)),
                  pl.BlockSpec(wp_b.shape, lambda i: (0, 0)),
                  row, row],
        out_specs=pl.BlockSpec((tm, d_model), lambda i: (i, 0)),
        out_shape=jax.ShapeDtypeStruct((seq, d_model), F32),
        compiler_params=_params(1),
        name="ple",
    )(x2, p2d, wg_b, wp_b, g, b)


def kernel(x, p, w_in, w_out, lambda_q1, lambda_k1, lambda_q2, lambda_k2, diff_subln_g, ln_g, ln_b,
           peer_wq, peer_keys, peer_u, peer_v, ple_gate, ple_proj):
    batch, seq, d_model = x.shape
    assert batch == 1 and w_in.shape[0] == DEPTH
    alpha = (2.0 * DEPTH) ** 0.25
    x2d = x[0]
    for i in range(DEPTH):
        proj = _in_proj(x2d.astype(BF16), w_in[i].astype(BF16), seq)
        od = _diff_attn(proj, lambda_q1[i][None], lambda_k1[i][None], lambda_q2[i][None],
                        lambda_k2[i][None], diff_subln_g[i][None], seq, i)
        osb = _sb_attn(proj, seq)
        x1, x1t = _out_proj(od, osb, w_out[i].astype(BF16), x2d, ln_g[i, 0][None], ln_b[i, 0][None],
                            alpha)
        rank, cnt, ea, ebz = _peer_route(x1, peer_wq[i].astype(BF16), peer_keys[i].astype(BF16))
        x2 = _peer_dense(x1t, peer_u[i].astype(BF16), peer_v[i].T.astype(BF16), rank, cnt, ea, ebz, x1,
                         ln_g[i, 1][None], ln_b[i, 1][None], alpha)
        x2d = _ple(x2, p[i, 0], ple_gate[i].astype(BF16), ple_proj[i].astype(BF16),
                   ln_g[i, 2][None], ln_b[i, 2][None], alpha)
    return x2d[None]
```

```python
import functools
import math

import jax
import jax.numpy as jnp
import numpy as np
from jax import lax
from jax.experimental import pallas as pl
from jax.experimental.pallas import tpu as pltpu

F32 = jnp.float32
BF16 = jnp.bfloat16

LANES = 128
V7X_VMEM_BYTES = 64 * 1024 * 1024
VMEM_LIMIT = V7X_VMEM_BYTES * 7 // 8

DIFF_HEADS = 8
SB_HEADS = 8
HEAD_DIM = 128
DIFF_QK_DIM = HEAD_DIM // 2
ROT_DIM = DIFF_QK_DIM // 4
ROPE_THETA = 500000.0
PEER_HEADS = 8
PEER_NKEYS = 128
PEER_HALF = 128
PEER_TOPK = 16
LN_EPS = 1e-5
DEPTH = 1
ATTN_Q_TILE = 512
ATTN_K_TILE = 256
ATTN_GROUP = 4

NT_DIMS = (((1,), (1,)), ((), ()))


def _params(n_grid):
    return pltpu.CompilerParams(dimension_semantics=("arbitrary",) * n_grid,
                                vmem_limit_bytes=VMEM_LIMIT)


def _layer_norm(y, g, b):
    mu = jnp.mean(y, axis=-1, keepdims=True)
    yc = y - mu
    var = jnp.mean(yc * yc, axis=-1, keepdims=True)
    return yc * lax.rsqrt(var + LN_EPS) * g + b


def _in_proj_kernel(x_ref, w_ref, c_ref, s1_ref, s2_ref, o_ref, *, n_rope_tiles):
    j = pl.program_id(0)
    acc = jnp.dot(x_ref[...], w_ref[...], preferred_element_type=F32)

    @pl.when(j >= n_rope_tiles)
    def _():
        o_ref[...] = acc.astype(o_ref.dtype)

    @pl.when(j < n_rope_tiles)
    def _():
        c = c_ref[0]
        s1 = s1_ref[0]
        s2 = s2_ref[0]
        for hb in range(acc.shape[1] // LANES):
            t = acc[:, hb * LANES:(hb + 1) * LANES]
            half = ROT_DIM // 2
            r = (t * c + pltpu.roll(t, half, axis=1) * s1
                 + pltpu.roll(t, LANES - half, axis=1) * s2)
            o_ref[:, hb * LANES:(hb + 1) * LANES] = r.astype(o_ref.dtype)


def _rope_lane_tables(seq):
    half = ROT_DIM // 2
    lane = np.arange(LANES) % DIFF_QK_DIM
    holds_a = lane < half
    holds_b = (lane >= half) & (lane < ROT_DIM)
    pos = jnp.arange(seq, dtype=F32)
    inv = ROPE_THETA ** (-jnp.arange(0, ROT_DIM, 2, dtype=F32) / ROT_DIM)
    ang = pos[:, None] * inv[lane % half][None, :]
    cos, sin = jnp.cos(ang), jnp.sin(ang)
    tabs = (jnp.where(holds_a | holds_b, cos, 1.0), jnp.where(holds_b, sin, 0.0),
            jnp.where(holds_a, -sin, 0.0))
    qscale = DIFF_QK_DIM ** -0.5
    return [jnp.stack([t * qscale, t]) for t in tabs]


def _in_proj(xb, w_in_b, seq):
    d_model = xb.shape[1]
    n_cols = w_in_b.shape[1]
    tn = DIFF_HEADS * HEAD_DIM
    tm = min(2048, seq)
    c_tab, s1_tab, s2_tab = _rope_lane_tables(seq)
    tab_spec = pl.BlockSpec((1, tm, LANES), lambda j, i: (jnp.minimum(j, 1), i, 0))
    return pl.pallas_call(
        functools.partial(_in_proj_kernel, n_rope_tiles=2),
        grid=(n_cols // tn, seq // tm),
        in_specs=[pl.BlockSpec((tm, d_model), lambda j, i: (i, 0)),
                  pl.BlockSpec((d_model, tn), lambda j, i: (0, j)),
                  tab_spec, tab_spec, tab_spec],
        out_specs=pl.BlockSpec((tm, tn), lambda j, i: (i, j)),
        out_shape=jax.ShapeDtypeStruct((seq, n_cols), BF16),
        compiler_params=_params(2),
        name="in_proj",
    )(xb, w_in_b, c_tab, s1_tab, s2_tab)


def _wavefront(n_streams, stages):
    vals = [None] * n_streams
    for t in range(n_streams + len(stages) - 1):
        for j in reversed(range(len(stages))):
            s = t - j
            if 0 <= s < n_streams:
                vals[s] = stages[j](s, vals[s])
    return tuple(vals)


def _diff_attn_kernel(q_ref, k_ref, v_ref, lq1_ref, lk1_ref, lq2_ref, lk2_ref, g_ref, o_ref,
                      m_ref, l_ref, acc_ref, *, tq, tk, lambda_init, group):
    qi = pl.program_id(1)
    ratio = tq // tk
    lane = lax.broadcasted_iota(jnp.int32, (tq, HEAD_DIM), 1)
    qs = []
    for g in range(group):
        q = q_ref[:, g * HEAD_DIM:(g + 1) * HEAD_DIM]
        zero = jnp.zeros_like(q)
        qs.append(jnp.where(lane < DIFF_QK_DIM, q, zero))
        qs.append(jnp.where(lane < DIFF_QK_DIM, zero, q))
    row = lax.broadcasted_iota(jnp.int32, (tq, tk), 0)
    col = lax.broadcasted_iota(jnp.int32, (tq, tk), 1)

    n_streams = 2 * group

    def chunks(kis, valid):
        starts = [pl.multiple_of(ki * tk, tk) for ki in kis]

        def cols(w):
            head = w % n_streams // 2
            return slice(head * HEAD_DIM, (head + 1) * HEAD_DIM)

        def qk(w, _):
            return lax.dot_general(qs[w % n_streams], k_ref[pl.ds(starts[w // n_streams], tk), cols(w)],
                                   NT_DIMS, preferred_element_type=F32)

        def softmax(w, sc):
            s = w % n_streams
            if valid is not None:
                sc = jnp.where(valid, sc, -jnp.inf)
            m = m_ref[s]
            m_new = jnp.maximum(m, jnp.max(sc, axis=-1, keepdims=True))
            p = jnp.exp(sc - jnp.concatenate([m_new] * (tk // LANES), axis=1))
            a = jnp.exp(m - m_new)
            m_ref[s] = m_new
            l_ref[s] = a * l_ref[s] + jnp.sum(p, axis=-1, keepdims=True)
            return a, p.astype(BF16)

        def pv(w, x):
            a, p = x
            s = w % n_streams
            acc_ref[s] = a * acc_ref[s] + jnp.dot(p, v_ref[pl.ds(starts[w // n_streams], tk), cols(w)],
                                                  preferred_element_type=F32)

        _wavefront(len(kis) * n_streams, (qk, softmax, pv))

    m_ref[...] = jnp.full(m_ref.shape, -1e30, F32)
    l_ref[...] = jnp.zeros(l_ref.shape, F32)
    acc_ref[...] = jnp.zeros(acc_ref.shape, F32)
    for d in range(ratio):
        chunks([qi * ratio + d], col + d * tk <= row)

    def body(step, carry):
        chunks([step * ratio + d for d in range(ratio)], None)
        return carry

    lax.fori_loop(0, qi, body, 0)

    lam = (jnp.exp(jnp.sum(lq1_ref[...] * lk1_ref[...], axis=-1, keepdims=True))
           - jnp.exp(jnp.sum(lq2_ref[...] * lk2_ref[...], axis=-1, keepdims=True))
           + lambda_init)
    for g in range(group):
        od = acc_ref[2 * g] / l_ref[2 * g] - lam * (acc_ref[2 * g + 1] / l_ref[2 * g + 1])
        od = od * lax.rsqrt(jnp.mean(od * od, axis=-1, keepdims=True) + LN_EPS)
        od = od * g_ref[...] * (1.0 - lambda_init)
        o_ref[:, g * HEAD_DIM:(g + 1) * HEAD_DIM] = od.astype(o_ref.dtype)


def _diff_attn(proj, lq1, lk1, lq2, lk2, g, seq, layer):
    tq = min(ATTN_Q_TILE, seq)
    tk = min(ATTN_K_TILE, seq)
    gw = ATTN_GROUP * HEAD_DIM
    n_groups = DIFF_HEADS // ATTN_GROUP
    lambda_init = 0.8 - 0.6 * math.exp(-0.3 * layer)
    vec = pl.BlockSpec((1, DIFF_QK_DIM), lambda h, i: (0, 0))
    once = pl.Buffered(1)
    return pl.pallas_call(
        functools.partial(_diff_attn_kernel, tq=tq, tk=tk, lambda_init=lambda_init,
                          group=ATTN_GROUP),
        grid=(n_groups, seq // tq),
        in_specs=[pl.BlockSpec((tq, gw), lambda h, i: (i, h)),
                  pl.BlockSpec((seq, gw), lambda h, i: (0, n_groups + h), pipeline_mode=once),
                  pl.BlockSpec((seq, gw), lambda h, i: (0, 2 * n_groups + h), pipeline_mode=once),
                  vec, vec, vec, vec,
                  pl.BlockSpec((1, HEAD_DIM), lambda h, i: (0, 0))],
        out_specs=pl.BlockSpec((tq, gw), lambda h, i: (i, h)),
        out_shape=jax.ShapeDtypeStruct((seq, DIFF_HEADS * HEAD_DIM), BF16),
        scratch_shapes=[pltpu.VMEM((2 * ATTN_GROUP, tq, LANES), F32),
                        pltpu.VMEM((2 * ATTN_GROUP, tq, LANES), F32),
                        pltpu.VMEM((2 * ATTN_GROUP, tq, HEAD_DIM), F32)],
        compiler_params=_params(2),
        name="diff_attn",
    )(proj, proj, proj, lq1, lk1, lq2, lk2, g)


def _sb_attn_kernel(q_ref, k_ref, v_ref, o_ref, drop_ref, acc_ref, *, tq, tk, scale, group):
    qi = pl.program_id(1)
    ratio = tq // tk
    tri = jnp.where(lax.broadcasted_iota(jnp.int32, (tk, tk), 0)
                    > lax.broadcasted_iota(jnp.int32, (tk, tk), 1), 1.0, 0.0).astype(BF16)
    tri2 = jnp.concatenate([tri, tri], axis=0)
    scale2 = scale * math.log2(math.e)
    row = lax.broadcasted_iota(jnp.int32, (tq, tk), 0)
    col = lax.broadcasted_iota(jnp.int32, (tq, tk), 1)

    def chunks(kis, valid):
        starts = [pl.multiple_of(ki * tk, tk) for ki in kis]

        def cols(w):
            return slice((w % group) * HEAD_DIM, (w % group + 1) * HEAD_DIM)

        def qk(w, _):
            return lax.dot_general(q_ref[:, cols(w)], k_ref[pl.ds(starts[w // group], tk), cols(w)],
                                   NT_DIMS, preferred_element_type=F32) * scale2

        def gates(w, z):
            g = w % group
            z_neg = jnp.minimum(z, 0.0)
            z_pos = jnp.maximum(z, 0.0)
            sp = jnp.log2(1.0 + jnp.exp2(z_neg - z_pos))
            log_beta = z_neg - sp
            skip = z_pos + sp
            if valid is not None:
                skip = jnp.where(valid, skip, 0.0)
            hi = skip.astype(BF16)
            lo = (skip - hi.astype(F32)).astype(BF16)
            drop = drop_ref[g]
            drop_ref[g] = drop + jnp.sum(skip, axis=-1, keepdims=True)
            return log_beta, drop, jnp.concatenate([hi, lo], axis=1)

        def cumsum(w, x):
            log_beta, drop, split = x
            return log_beta, drop, jnp.dot(split, tri2, preferred_element_type=F32)

        def weights(w, x):
            log_beta, drop, suffix = x
            a = jnp.exp2(log_beta - suffix - jnp.concatenate([drop] * (tk // LANES), axis=1))
            if valid is not None:
                a = jnp.where(valid, a, 0.0)
            return a.astype(BF16)

        def pv(w, a):
            g = w % group
            acc_ref[g] += jnp.dot(a, v_ref[pl.ds(starts[w // group], tk), cols(w)],
                                  preferred_element_type=F32)

        _wavefront(len(kis) * group, (qk, gates, cumsum, weights, pv))

    drop_ref[...] = jnp.zeros(drop_ref.shape, F32)
    acc_ref[...] = jnp.zeros(acc_ref.shape, F32)
    for d in reversed(range(ratio)):
        chunks([qi * ratio + d], col + d * tk < row)

    def body(step, carry):
        chunks([(qi - step) * ratio - 1 - d for d in range(ratio)], None)
        return carry

    lax.fori_loop(0, qi, body, 0)
    for g in range(group):
        o_ref[:, g * HEAD_DIM:(g + 1) * HEAD_DIM] = acc_ref[g].astype(o_ref.dtype)


def _sb_attn(proj, seq):
    tq = min(ATTN_Q_TILE, seq)
    tk = min(ATTN_K_TILE, seq)
    gw = ATTN_GROUP * HEAD_DIM
    n_groups = SB_HEADS // ATTN_GROUP
    base = 3 * DIFF_HEADS // ATTN_GROUP
    once = pl.Buffered(1)
    return pl.pallas_call(
        functools.partial(_sb_attn_kernel, tq=tq, tk=tk, scale=HEAD_DIM ** -0.5, group=ATTN_GROUP),
        grid=(n_groups, seq // tq),
        in_specs=[pl.BlockSpec((tq, gw), lambda h, i: (i, base + h)),
                  pl.BlockSpec((seq, gw), lambda h, i: (0, base + n_groups + h), pipeline_mode=once),
                  pl.BlockSpec((seq, gw), lambda h, i: (0, base + 2 * n_groups + h),
                               pipeline_mode=once)],
        out_specs=pl.BlockSpec((tq, gw), lambda h, i: (i, h)),
        out_shape=jax.ShapeDtypeStruct((seq, SB_HEADS * HEAD_DIM), BF16),
        scratch_shapes=[pltpu.VMEM((ATTN_GROUP, tq, LANES), F32),
                        pltpu.VMEM((ATTN_GROUP, tq, HEAD_DIM), F32)],
        compiler_params=_params(2),
        name="sb_attn",
    )(proj, proj, proj)


def _out_proj_kernel(od_ref, osb_ref, w_ref, x_ref, g_ref, b_ref, o_ref, ot_ref, *, alpha):
    n_diff = od_ref.shape[1]
    y = jnp.dot(od_ref[...], w_ref[:n_diff, :], preferred_element_type=F32)
    y = y + jnp.dot(osb_ref[...], w_ref[n_diff:, :], preferred_element_type=F32)
    y = y + alpha * x_ref[...]
    out = _layer_norm(y, g_ref[...], b_ref[...])
    o_ref[...] = out
    ot_ref[...] = out.T.astype(ot_ref.dtype)


def _out_proj(od, osb, w_out_b, x2d, g, b, alpha):
    seq, d_model = x2d.shape
    tm = min(512, seq)
    row = pl.BlockSpec((1, d_model), lambda i: (0, 0))
    return pl.pallas_call(
        functools.partial(_out_proj_kernel, alpha=alpha),
        grid=(seq // tm,),
        in_specs=[pl.BlockSpec((tm, od.shape[1]), lambda i: (i, 0)),
                  pl.BlockSpec((tm, osb.shape[1]), lambda i: (i, 0)),
                  pl.BlockSpec(w_out_b.shape, lambda i: (0, 0)),
                  pl.BlockSpec((tm, d_model), lambda i: (i, 0)),
                  row, row],
        out_specs=[pl.BlockSpec((tm, d_model), lambda i: (i, 0)),
                   pl.BlockSpec((d_model, tm), lambda i: (0, i))],
        out_shape=[jax.ShapeDtypeStruct((seq, d_model), F32),
                   jax.ShapeDtypeStruct((d_model, seq), BF16)],
        compiler_params=_params(1),
        name="out_proj",
    )(od, osb, w_out_b, x2d, g, b)


def _top_values(s, k):
    out = []
    for _ in range(k):
        m = jnp.max(s, axis=0, keepdims=True)
        out.append(m)
        s = jnp.where(s == m, -jnp.inf, s)
    return out


def _peer_route_kernel(x_ref, wq_ref, keys_ref, rank_ref, cnt_ref, ea_ref, ebz_ref):
    q = jnp.dot(x_ref[...].astype(BF16), wq_ref[...], preferred_element_type=F32).astype(BF16)
    half = PEER_TOPK // 2
    for h in range(PEER_HEADS):
        qa = q[:, (2 * h) * PEER_HALF:(2 * h + 1) * PEER_HALF]
        qb = q[:, (2 * h + 1) * PEER_HALF:(2 * h + 2) * PEER_HALF]
        sa = lax.dot_general(keys_ref[0], qa, NT_DIMS, preferred_element_type=F32)
        sb = lax.dot_general(keys_ref[1], qb, NT_DIMS, preferred_element_type=F32)
        va = _top_values(sa, PEER_TOPK)
        vb = _top_values(sb, PEER_TOPK)
        va_all = jnp.concatenate(va, axis=0)
        vb_all = jnp.concatenate(vb, axis=0)
        cand = jnp.concatenate([va[0] + vb_all]
                               + [va[a] + vb_all[:half] for a in range(1, half)]
                               + [va_all[half:] + vb[0]], axis=0)
        top = _top_values(cand, PEER_TOPK)
        tau = top[PEER_TOPK - 1]
        z = top[0] * 0.0
        for tv in top:
            z = z + jnp.exp(tv - top[0])
        sel = jnp.where(cand >= tau, 1.0, 0.0)
        cnt = jnp.zeros(sa.shape, F32)
        rank = jnp.zeros(sb.shape, F32)
        for a in range(PEER_TOPK):
            if a == 0:
                cnt_a = jnp.sum(sel[:PEER_TOPK], axis=0, keepdims=True)
            elif a < half:
                cnt_a = jnp.sum(sel[PEER_TOPK + (a - 1) * half:PEER_TOPK + a * half], axis=0,
                                keepdims=True)
            else:
                tail = PEER_TOPK + (half - 1) * half
                cnt_a = sel[tail + a - half:tail + a - half + 1]
            cnt = cnt + jnp.where(sa == va[a], cnt_a, 0.0)
            rank = rank + jnp.where(sb < vb[a], 1.0, 0.0)
        rank_ref[h] = rank.astype(rank_ref.dtype)
        cnt_ref[h] = cnt
        ea_ref[h] = jnp.exp(sa - va[0])
        ebz_ref[h] = (jnp.exp(sb - vb[0]) / z).astype(ebz_ref.dtype)


def _peer_route(x1, wq_b, keys_b):
    seq, d_model = x1.shape
    tt = min(256, seq)
    shape = (PEER_HEADS, PEER_NKEYS, seq)
    ospec = pl.BlockSpec((PEER_HEADS, PEER_NKEYS, tt), lambda i: (0, 0, i))
    return pl.pallas_call(
        _peer_route_kernel,
        grid=(seq // tt,),
        in_specs=[pl.BlockSpec((tt, d_model), lambda i: (i, 0)),
                  pl.BlockSpec(wq_b.shape, lambda i: (0, 0)),
                  pl.BlockSpec(keys_b.shape, lambda i: (0, 0, 0))],
        out_specs=[ospec] * 4,
        out_shape=[jax.ShapeDtypeStruct(shape, BF16), jax.ShapeDtypeStruct(shape, F32),
                   jax.ShapeDtypeStruct(shape, F32), jax.ShapeDtypeStruct(shape, BF16)],
        compiler_params=_params(1),
        name="peer_route",
    )(x1, wq_b, keys_b)


def _gelu_tanh(x):
    c = math.sqrt(2.0 / math.pi)
    return 0.5 * x * (1.0 + jnp.tanh(c * (x + 0.044715 * (x * x * x))))


def _bf16_rows(row, n):
    packed = jnp.broadcast_to(row, (16, row.shape[1])).astype(BF16)
    return jnp.concatenate([packed] * (n // 16), axis=0)


def _peer_dense_kernel(xt_ref, u_ref, vt_ref, rank_ref, cnt_ref, ea_ref, ebz_ref, x1_ref, g_ref, b_ref,
                       o_ref, acc_ref, *, alpha, sub):
    c = pl.program_id(1)
    n_sub = u_ref.shape[0] // sub
    groups = sub // PEER_NKEYS

    @pl.when(c == 0)
    def _():
        acc_ref[...] = jnp.zeros_like(acc_ref)

    def hidden(k, _):
        return jnp.dot(u_ref[k * sub:(k + 1) * sub, :], xt_ref[...], preferred_element_type=F32)

    def weights(k, ht):
        out = []
        for ii in range(groups):
            i = (c * n_sub + k) * groups + ii
            gate = jnp.zeros((PEER_NKEYS, ht.shape[1]), BF16)
            for h in range(PEER_HEADS):
                cnt_row = _bf16_rows(cnt_ref[h, pl.ds(i, 1), :], PEER_NKEYS)
                ea_row = _bf16_rows(ea_ref[h, pl.ds(i, 1), :], PEER_NKEYS)
                gate = gate + jnp.where(rank_ref[h] < cnt_row, ebz_ref[h], jnp.zeros_like(gate)) * ea_row
            act = _gelu_tanh(ht[ii * PEER_NKEYS:(ii + 1) * PEER_NKEYS])
            out.append(gate * act.astype(BF16))
        return jnp.concatenate(out, axis=0)

    def project(k, w):
        acc_ref[...] += jnp.dot(vt_ref[:, k * sub:(k + 1) * sub], w, preferred_element_type=F32)

    _wavefront(n_sub, (hidden, weights, project))

    @pl.when(c == pl.num_programs(1) - 1)
    def _():
        y = acc_ref[...].T + alpha * x1_ref[...]
        o_ref[...] = _layer_norm(y, g_ref[...], b_ref[...])


def _peer_dense(x1t, u_b, vt_b, rank, cnt, ea, ebz, x1, g, b, alpha):
    seq, d_model = x1.shape
    n_exp = u_b.shape[0]
    tt = min(512, seq)
    ec = 1024
    once = pl.Buffered(1)
    gspec = pl.BlockSpec((PEER_HEADS, PEER_NKEYS, tt), lambda i, c: (0, 0, i), pipeline_mode=once)
    row = pl.BlockSpec((1, d_model), lambda i, c: (0, 0))
    return pl.pallas_call(
        functools.partial(_peer_dense_kernel, alpha=alpha, sub=ec // 2),
        grid=(seq // tt, n_exp // ec),
        in_specs=[pl.BlockSpec((d_model, tt), lambda i, c: (0, i), pipeline_mode=once),
                  pl.BlockSpec((ec, d_model), lambda i, c: (c, 0)),
                  pl.BlockSpec((d_model, ec), lambda i, c: (0, c)),
                  gspec, gspec, gspec, gspec,
                  pl.BlockSpec((tt, d_model), lambda i, c: (i, 0), pipeline_mode=once),
                  row, row],
        out_specs=pl.BlockSpec((tt, d_model), lambda i, c: (i, 0)),
        out_shape=jax.ShapeDtypeStruct((seq, d_model), F32),
        scratch_shapes=[pltpu.VMEM((d_model, tt), F32)],
        compiler_params=_params(2),
        name="peer_dense",
    )(x1t, u_b, vt_b, rank, cnt, ea, ebz, x1, g, b)


def _ple_kernel(x_ref, p_ref, wg_ref, wp_ref, g_ref, b_ref, o_ref, *, alpha):
    x = x_ref[...]
    gate = jax.nn.sigmoid(jnp.dot(x.astype(BF16), wg_ref[...], preferred_element_type=F32))
    emb = jnp.dot(p_ref[...].astype(BF16), wp_ref[...], preferred_element_type=F32)
    o_ref[...] = _layer_norm(alpha * x + gate * emb, g_ref[...], b_ref[...])


def _ple(x2, p2d, wg_b, wp_b, g, b, alpha):
    seq, d_model = x2.shape
    tm = min(512, seq)
    row = pl.BlockSpec((1, d_model), lambda i: (0, 0))
    return pl.pallas_call(
        functools.partial(_ple_kernel, alpha=alpha),
        grid=(seq // tm,),
        in_specs=[pl.BlockSpec((tm, d_model), lambda i: (i, 0)),
                  pl.BlockSpec((tm, p2d.shape[1]), lambda i: (i, 0)),
                  pl.BlockSpec(wg_b.shape, lambda i: (0, 0)),
                  pl.BlockSpec(wp_b.shape, lambda i: (0, 0)),
                  row, row],
        out_specs=pl.BlockSpec((tm, d_model), lambda i: (i, 0)),
        out_shape=jax.ShapeDtypeStruct((seq, d_model), F32),
        compiler_params=_params(1),
        name="ple",
    )(x2, p2d, wg_b, wp_b, g, b)


def kernel(x, p, w_in, w_out, lambda_q1, lambda_k1, lambda_q2, lambda_k2, diff_subln_g, ln_g, ln_b,
           peer_wq, peer_keys, peer_u, peer_v, ple_gate, ple_proj):
    batch, seq, d_model = x.shape
    assert batch == 1 and w_in.shape[0] == DEPTH
    alpha = (2.0 * DEPTH) ** 0.25
    x2d = x[0]
    for i in range(DEPTH):
        proj = _in_proj(x2d.astype(BF16), w_in[i].astype(BF16), seq)
        od = _diff_attn(proj, lambda_q1[i][None], lambda_k1[i][None], lambda_q2[i][None],
                        lambda_k2[i][None], diff_subln_g[i][None], seq, i)
        osb = _sb_attn(proj, seq)
        x1, x1t = _out_proj(od, osb, w_out[i].astype(BF16), x2d, ln_g[i, 0][None], ln_b[i, 0][None],
                            alpha)
        rank, cnt, ea, ebz = _peer_route(x1, peer_wq[i].astype(BF16), peer_keys[i].astype(BF16))
        x2 = _peer_dense(x1t, peer_u[i].astype(BF16), peer_v[i].T.astype(BF16), rank, cnt, ea, ebz, x1,
                         ln_g[i, 1][None], ln_b[i, 1][None], alpha)
        x2d = _ple(x2, p[i, 0], ple_gate[i].astype(BF16), ple_proj[i].astype(BF16),
                   ln_g[i, 2][None], ln_b[i, 2][None], alpha)
    return x2d[None]
```

```python
import functools
import math

import jax
import jax.numpy as jnp
import numpy as np
from jax import lax
from jax.experimental import pallas as pl
from jax.experimental.pallas import tpu as pltpu

F32 = jnp.float32
BF16 = jnp.bfloat16

LANES = 128
V7X_VMEM_BYTES = 64 * 1024 * 1024
VMEM_LIMIT = V7X_VMEM_BYTES * 7 // 8

DIFF_HEADS = 8
SB_HEADS = 8
HEAD_DIM = 128
DIFF_QK_DIM = HEAD_DIM // 2
ROT_DIM = DIFF_QK_DIM // 4
ROPE_THETA = 500000.0
PEER_HEADS = 8
PEER_NKEYS = 128
PEER_HALF = 128
PEER_TOPK = 16
LN_EPS = 1e-5
DEPTH = 1
ATTN_Q_TILE = 512
SB_K_TILE = 256
DIFF_K_TILE = 512
ATTN_GROUP = 4

NT_DIMS = (((1,), (1,)), ((), ()))


def _params(n_grid):
    return pltpu.CompilerParams(dimension_semantics=("arbitrary",) * n_grid,
                                vmem_limit_bytes=VMEM_LIMIT)


def _layer_norm(y, g, b):
    mu = jnp.mean(y, axis=-1, keepdims=True)
    yc = y - mu
    var = jnp.mean(yc * yc, axis=-1, keepdims=True)
    return yc * lax.rsqrt(var + LN_EPS) * g + b


def _in_proj_kernel(x_ref, w_ref, c_ref, s1_ref, s2_ref, o_ref, *, n_rope_tiles, sb_q_tile, sb_q_scale):
    j = pl.program_id(0)
    acc = jnp.dot(x_ref[...], w_ref[...], preferred_element_type=F32)

    @pl.when((j >= n_rope_tiles) & (j != sb_q_tile))
    def _():
        o_ref[...] = acc.astype(o_ref.dtype)

    @pl.when(j == sb_q_tile)
    def _():
        o_ref[...] = (acc * sb_q_scale).astype(o_ref.dtype)

    @pl.when(j < n_rope_tiles)
    def _():
        c = c_ref[0]
        s1 = s1_ref[0]
        s2 = s2_ref[0]
        for hb in range(acc.shape[1] // LANES):
            t = acc[:, hb * LANES:(hb + 1) * LANES]
            half = ROT_DIM // 2
            r = (t * c + pltpu.roll(t, half, axis=1) * s1
                 + pltpu.roll(t, LANES - half, axis=1) * s2)
            o_ref[:, hb * LANES:(hb + 1) * LANES] = r.astype(o_ref.dtype)


def _rope_lane_tables(seq):
    half = ROT_DIM // 2
    lane = np.arange(LANES) % DIFF_QK_DIM
    holds_a = lane < half
    holds_b = (lane >= half) & (lane < ROT_DIM)
    pos = jnp.arange(seq, dtype=F32)
    inv = ROPE_THETA ** (-jnp.arange(0, ROT_DIM, 2, dtype=F32) / ROT_DIM)
    ang = pos[:, None] * inv[lane % half][None, :]
    cos, sin = jnp.cos(ang), jnp.sin(ang)
    tabs = (jnp.where(holds_a | holds_b, cos, 1.0), jnp.where(holds_b, sin, 0.0),
            jnp.where(holds_a, -sin, 0.0))
    qscale = DIFF_QK_DIM ** -0.5
    return [jnp.stack([t * qscale, t]) for t in tabs]


def _in_proj(xb, w_in_b, seq):
    d_model = xb.shape[1]
    n_cols = w_in_b.shape[1]
    tn = DIFF_HEADS * HEAD_DIM
    tm = min(2048, seq)
    c_tab, s1_tab, s2_tab = _rope_lane_tables(seq)
    tab_spec = pl.BlockSpec((1, tm, LANES), lambda j, i: (jnp.minimum(j, 1), i, 0))
    return pl.pallas_call(
        functools.partial(_in_proj_kernel, n_rope_tiles=2, sb_q_tile=3,
                          sb_q_scale=HEAD_DIM ** -0.5 * math.log2(math.e)),
        grid=(n_cols // tn, seq // tm),
        in_specs=[pl.BlockSpec((tm, d_model), lambda j, i: (i, 0)),
                  pl.BlockSpec((d_model, tn), lambda j, i: (0, j)),
                  tab_spec, tab_spec, tab_spec],
        out_specs=pl.BlockSpec((tm, tn), lambda j, i: (i, j)),
        out_shape=jax.ShapeDtypeStruct((seq, n_cols), BF16),
        compiler_params=_params(2),
        name="in_proj",
    )(xb, w_in_b, c_tab, s1_tab, s2_tab)


def _wavefront(n_streams, stages):
    vals = [None] * n_streams
    for t in range(n_streams + len(stages) - 1):
        for j in reversed(range(len(stages))):
            s = t - j
            if 0 <= s < n_streams:
                vals[s] = stages[j](s, vals[s])
    return tuple(vals)


def _diff_attn_kernel(q_ref, k_ref, v_ref, lq1_ref, lk1_ref, lq2_ref, lk2_ref, g_ref, o_ref,
                      m_ref, l_ref, acc_ref, *, tq, tk, lambda_init, group):
    qi = pl.program_id(1)
    ratio = tq // tk
    lane = lax.broadcasted_iota(jnp.int32, (tq, HEAD_DIM), 1)
    qs = []
    for g in range(group):
        q = q_ref[:, g * HEAD_DIM:(g + 1) * HEAD_DIM]
        zero = jnp.zeros_like(q)
        qs.append(jnp.where(lane < DIFF_QK_DIM, q, zero))
        qs.append(jnp.where(lane < DIFF_QK_DIM, zero, q))
    row = lax.broadcasted_iota(jnp.int32, (tq, tk), 0)
    col = lax.broadcasted_iota(jnp.int32, (tq, tk), 1)

    n_streams = 2 * group

    def chunks(kis, valid):
        starts = [pl.multiple_of(ki * tk, tk) for ki in kis]

        def cols(w):
            head = w % n_streams // 2
            return slice(head * HEAD_DIM, (head + 1) * HEAD_DIM)

        def qk(w, _):
            return lax.dot_general(qs[w % n_streams], k_ref[pl.ds(starts[w // n_streams], tk), cols(w)],
                                   NT_DIMS, preferred_element_type=F32)

        def softmax(w, sc):
            s = w % n_streams
            if valid is not None:
                sc = jnp.where(valid, sc, -jnp.inf)
            m = m_ref[s]
            m_new = jnp.maximum(m, jnp.max(sc, axis=-1, keepdims=True))
            p = jnp.exp(sc - jnp.concatenate([m_new] * (tk // LANES), axis=1))
            a = jnp.exp(m - m_new)
            m_ref[s] = m_new
            l_ref[s] = a * l_ref[s] + jnp.sum(p, axis=-1, keepdims=True)
            return a, p.astype(BF16)

        def pv(w, x):
            a, p = x
            s = w % n_streams
            acc_ref[s] = a * acc_ref[s] + jnp.dot(p, v_ref[pl.ds(starts[w // n_streams], tk), cols(w)],
                                                  preferred_element_type=F32)

        _wavefront(len(kis) * n_streams, (qk, softmax, pv))

    m_ref[...] = jnp.full(m_ref.shape, -1e30, F32)
    l_ref[...] = jnp.zeros(l_ref.shape, F32)
    acc_ref[...] = jnp.zeros(acc_ref.shape, F32)
    for d in range(ratio):
        chunks([qi * ratio + d], col + d * tk <= row)

    def body(step, carry):
        chunks([step * ratio + d for d in range(ratio)], None)
        return carry

    lax.fori_loop(0, qi, body, 0)

    lam = (jnp.exp(jnp.sum(lq1_ref[...] * lk1_ref[...], axis=-1, keepdims=True))
           - jnp.exp(jnp.sum(lq2_ref[...] * lk2_ref[...], axis=-1, keepdims=True))
           + lambda_init)
    for g in range(group):
        od = acc_ref[2 * g] / l_ref[2 * g] - lam * (acc_ref[2 * g + 1] / l_ref[2 * g + 1])
        od = od * lax.rsqrt(jnp.mean(od * od, axis=-1, keepdims=True) + LN_EPS)
        od = od * g_ref[...] * (1.0 - lambda_init)
        o_ref[:, g * HEAD_DIM:(g + 1) * HEAD_DIM] = od.astype(o_ref.dtype)


def _diff_attn(proj, lq1, lk1, lq2, lk2, g, seq, layer):
    tq = min(ATTN_Q_TILE, seq)
    tk = min(DIFF_K_TILE, seq)
    gw = ATTN_GROUP * HEAD_DIM
    n_groups = DIFF_HEADS // ATTN_GROUP
    lambda_init = 0.8 - 0.6 * math.exp(-0.3 * layer)
    vec = pl.BlockSpec((1, DIFF_QK_DIM), lambda h, i: (0, 0))
    once = pl.Buffered(1)
    return pl.pallas_call(
        functools.partial(_diff_attn_kernel, tq=tq, tk=tk, lambda_init=lambda_init,
                          group=ATTN_GROUP),
        grid=(n_groups, seq // tq),
        in_specs=[pl.BlockSpec((tq, gw), lambda h, i: (i, h)),
                  pl.BlockSpec((seq, gw), lambda h, i: (0, n_groups + h), pipeline_mode=once),
                  pl.BlockSpec((seq, gw), lambda h, i: (0, 2 * n_groups + h), pipeline_mode=once),
                  vec, vec, vec, vec,
                  pl.BlockSpec((1, HEAD_DIM), lambda h, i: (0, 0))],
        out_specs=pl.BlockSpec((tq, gw), lambda h, i: (i, h)),
        out_shape=jax.ShapeDtypeStruct((seq, DIFF_HEADS * HEAD_DIM), BF16),
        scratch_shapes=[pltpu.VMEM((2 * ATTN_GROUP, tq, LANES), F32),
                        pltpu.VMEM((2 * ATTN_GROUP, tq, LANES), F32),
                        pltpu.VMEM((2 * ATTN_GROUP, tq, HEAD_DIM), F32)],
        compiler_params=_params(2),
        name="diff_attn",
    )(proj, proj, proj, lq1, lk1, lq2, lk2, g)


def _sb_attn_kernel(q_ref, k_ref, v_ref, o_ref, drop_ref, acc_ref, *, tq, tk, group):
    qi = pl.program_id(1)
    ratio = tq // tk
    tri = jnp.where(lax.broadcasted_iota(jnp.int32, (tk, tk), 0)
                    > lax.broadcasted_iota(jnp.int32, (tk, tk), 1), 1.0, 0.0).astype(BF16)
    row = lax.broadcasted_iota(jnp.int32, (tq, tk), 0)
    col = lax.broadcasted_iota(jnp.int32, (tq, tk), 1)

    def chunks(kis, valid):
        starts = [pl.multiple_of(ki * tk, tk) for ki in kis]

        def cols(w):
            return slice((w % group) * HEAD_DIM, (w % group + 1) * HEAD_DIM)

        def qk(w, _):
            return lax.dot_general(q_ref[:, cols(w)], k_ref[pl.ds(starts[w // group], tk), cols(w)],
                                   NT_DIMS, preferred_element_type=F32)

        def gates(w, z):
            g = w % group
            sp = jnp.log2(1.0 + jnp.exp2(-jnp.abs(z)))
            log_beta = jnp.minimum(z, 0.0) - sp
            skip = z - log_beta
            if valid is not None:
                skip = jnp.where(valid, skip, 0.0)
            drop = drop_ref[g]
            drop_ref[g] = drop + jnp.sum(skip, axis=-1, keepdims=True)
            return log_beta, drop, skip.astype(BF16)

        def cumsum(w, x):
            log_beta, drop, skip = x
            return log_beta, drop, jnp.dot(skip, tri, preferred_element_type=F32)

        def weights(w, x):
            log_beta, drop, suffix = x
            a = jnp.exp2(log_beta - suffix - jnp.concatenate([drop] * (tk // LANES), axis=1))
            if valid is not None:
                a = jnp.where(valid, a, 0.0)
            return a.astype(BF16)

        def pv(w, a):
            g = w % group
            acc_ref[g] += jnp.dot(a, v_ref[pl.ds(starts[w // group], tk), cols(w)],
                                  preferred_element_type=F32)

        _wavefront(len(kis) * group, (qk, gates, cumsum, weights, pv))

    drop_ref[...] = jnp.zeros(drop_ref.shape, F32)
    acc_ref[...] = jnp.zeros(acc_ref.shape, F32)
    for d in reversed(range(ratio)):
        chunks([qi * ratio + d], col + d * tk < row)

    def body(step, carry):
        chunks([(qi - step) * ratio - 1 - d for d in range(ratio)], None)
        return carry

    lax.fori_loop(0, qi, body, 0)
    for g in range(group):
        o_ref[:, g * HEAD_DIM:(g + 1) * HEAD_DIM] = acc_ref[g].astype(o_ref.dtype)


def _sb_attn(proj, seq):
    tq = min(ATTN_Q_TILE, seq)
    tk = min(SB_K_TILE, seq)
    gw = ATTN_GROUP * HEAD_DIM
    n_groups = SB_HEADS // ATTN_GROUP
    base = 3 * DIFF_HEADS // ATTN_GROUP
    once = pl.Buffered(1)
    return pl.pallas_call(
        functools.partial(_sb_attn_kernel, tq=tq, tk=tk, group=ATTN_GROUP),
        grid=(n_groups, seq // tq),
        in_specs=[pl.BlockSpec((tq, gw), lambda h, i: (i, base + h)),
                  pl.BlockSpec((seq, gw), lambda h, i: (0, base + n_groups + h), pipeline_mode=once),
                  pl.BlockSpec((seq, gw), lambda h, i: (0, base + 2 * n_groups + h),
                               pipeline_mode=once)],
        out_specs=pl.BlockSpec((tq, gw), lambda h, i: (i, h)),
        out_shape=jax.ShapeDtypeStruct((seq, SB_HEADS * HEAD_DIM), BF16),
        scratch_shapes=[pltpu.VMEM((ATTN_GROUP, tq, LANES), F32),
                        pltpu.VMEM((ATTN_GROUP, tq, HEAD_DIM), F32)],
        compiler_params=_params(2),
        name="sb_attn",
    )(proj, proj, proj)


def _out_proj_kernel(od_ref, osb_ref, w_ref, x_ref, g_ref, b_ref, o_ref, ot_ref, *, alpha):
    n_diff = od_ref.shape[1]
    y = jnp.dot(od_ref[...], w_ref[:n_diff, :], preferred_element_type=F32)
    y = y + jnp.dot(osb_ref[...], w_ref[n_diff:, :], preferred_element_type=F32)
    y = y + alpha * x_ref[...]
    out = _layer_norm(y, g_ref[...], b_ref[...])
    o_ref[...] = out
    ot_ref[...] = out.T.astype(ot_ref.dtype)


def _out_proj(od, osb, w_out_b, x2d, g, b, alpha):
    seq, d_model = x2d.shape
    tm = min(512, seq)
    row = pl.BlockSpec((1, d_model), lambda i: (0, 0))
    return pl.pallas_call(
        functools.partial(_out_proj_kernel, alpha=alpha),
        grid=(seq // tm,),
        in_specs=[pl.BlockSpec((tm, od.shape[1]), lambda i: (i, 0)),
                  pl.BlockSpec((tm, osb.shape[1]), lambda i: (i, 0)),
                  pl.BlockSpec(w_out_b.shape, lambda i: (0, 0)),
                  pl.BlockSpec((tm, d_model), lambda i: (i, 0)),
                  row, row],
        out_specs=[pl.BlockSpec((tm, d_model), lambda i: (i, 0)),
                   pl.BlockSpec((d_model, tm), lambda i: (0, i))],
        out_shape=[jax.ShapeDtypeStruct((seq, d_model), F32),
                   jax.ShapeDtypeStruct((d_model, seq), BF16)],
        compiler_params=_params(1),
        name="out_proj",
    )(od, osb, w_out_b, x2d, g, b)


def _top_values(s, k):
    out = []
    for _ in range(k):
        m = jnp.max(s, axis=0, keepdims=True)
        out.append(m)
        s = jnp.where(s == m, -jnp.inf, s)
    return out


def _peer_route_kernel(x_ref, wq_ref, keys_ref, rank_ref, cnt_ref, ea_ref, ebz_ref):
    q = jnp.dot(x_ref[...].astype(BF16), wq_ref[...], preferred_element_type=F32).astype(BF16)
    half = PEER_TOPK // 2
    for h in range(PEER_HEADS):
        qa = q[:, (2 * h) * PEER_HALF:(2 * h + 1) * PEER_HALF]
        qb = q[:, (2 * h + 1) * PEER_HALF:(2 * h + 2) * PEER_HALF]
        sa = lax.dot_general(keys_ref[0], qa, NT_DIMS, preferred_element_type=F32)
        sb = lax.dot_general(keys_ref[1], qb, NT_DIMS, preferred_element_type=F32)
        va = _top_values(sa, PEER_TOPK)
        vb = _top_values(sb, PEER_TOPK)
        va_all = jnp.concatenate(va, axis=0)
        vb_all = jnp.concatenate(vb, axis=0)
        cand = jnp.concatenate([va[0] + vb_all]
                               + [va[a] + vb_all[:half] for a in range(1, half)]
                               + [va_all[half:] + vb[0]], axis=0)
        top = _top_values(cand, PEER_TOPK)
        tau = top[PEER_TOPK - 1]
        z = top[0] * 0.0
        for tv in top:
            z = z + jnp.exp(tv - top[0])
        sel = jnp.where(cand >= tau, 1.0, 0.0)
        cnt = jnp.zeros(sa.shape, F32)
        rank = jnp.zeros(sb.shape, F32)
        for a in range(PEER_TOPK):
            if a == 0:
                cnt_a = jnp.sum(sel[:PEER_TOPK], axis=0, keepdims=True)
            elif a < half:
                cnt_a = jnp.sum(sel[PEER_TOPK + (a - 1) * half:PEER_TOPK + a * half], axis=0,
                                keepdims=True)
            else:
                tail = PEER_TOPK + (half - 1) * half
                cnt_a = sel[tail + a - half:tail + a - half + 1]
            cnt = cnt + jnp.where(sa == va[a], cnt_a, 0.0)
            rank = rank + jnp.where(sb < vb[a], 1.0, 0.0)
        rank_ref[h] = rank.astype(rank_ref.dtype)
        cnt_ref[h] = cnt
        ea_ref[h] = jnp.exp(sa - va[0])
        ebz_ref[h] = (jnp.exp(sb - vb[0]) / z).astype(ebz_ref.dtype)


def _peer_route(x1, wq_b, keys_b):
    seq, d_model = x1.shape
    tt = min(256, seq)
    shape = (PEER_HEADS, PEER_NKEYS, seq)
    ospec = pl.BlockSpec((PEER_HEADS, PEER_NKEYS, tt), lambda i: (0, 0, i))
    return pl.pallas_call(
        _peer_route_kernel,
        grid=(seq // tt,),
        in_specs=[pl.BlockSpec((tt, d_model), lambda i: (i, 0)),
                  pl.BlockSpec(wq_b.shape, lambda i: (0, 0)),
                  pl.BlockSpec(keys_b.shape, lambda i: (0, 0, 0))],
        out_specs=[ospec] * 4,
        out_shape=[jax.ShapeDtypeStruct(shape, BF16), jax.ShapeDtypeStruct(shape, F32),
                   jax.ShapeDtypeStruct(shape, F32), jax.ShapeDtypeStruct(shape, BF16)],
        compiler_params=_params(1),
        name="peer_route",
    )(x1, wq_b, keys_b)


def _gelu_tanh(x):
    c = math.sqrt(2.0 / math.pi)
    return 0.5 * x * (1.0 + jnp.tanh(c * (x + 0.044715 * (x * x * x))))


def _bf16_rows(row, n):
    packed = jnp.broadcast_to(row, (16, row.shape[1])).astype(BF16)
    return jnp.concatenate([packed] * (n // 16), axis=0)


def _peer_dense_kernel(xt_ref, u_ref, vt_ref, rank_ref, cnt_ref, ea_ref, ebz_ref, x1_ref, g_ref, b_ref,
                       o_ref, acc_ref, *, alpha, sub):
    c = pl.program_id(1)
    n_sub = u_ref.shape[0] // sub
    groups = sub // PEER_NKEYS

    @pl.when(c == 0)
    def _():
        acc_ref[...] = jnp.zeros_like(acc_ref)

    def hidden(k, _):
        return jnp.dot(u_ref[k * sub:(k + 1) * sub, :], xt_ref[...], preferred_element_type=F32)

    def weights(k, ht):
        out = []
        for ii in range(groups):
            i = (c * n_sub + k) * groups + ii
            gate = jnp.zeros((PEER_NKEYS, ht.shape[1]), BF16)
            for h in range(PEER_HEADS):
                cnt_row = _bf16_rows(cnt_ref[h, pl.ds(i, 1), :], PEER_NKEYS)
                ea_row = _bf16_rows(ea_ref[h, pl.ds(i, 1), :], PEER_NKEYS)
                gate = gate + jnp.where(rank_ref[h] < cnt_row, ebz_ref[h], jnp.zeros_like(gate)) * ea_row
            act = _gelu_tanh(ht[ii * PEER_NKEYS:(ii + 1) * PEER_NKEYS])
            out.append(gate * act.astype(BF16))
        return jnp.concatenate(out, axis=0)

    def project(k, w):
        acc_ref[...] += jnp.dot(vt_ref[:, k * sub:(k + 1) * sub], w, preferred_element_type=F32)

    _wavefront(n_sub, (hidden, weights, project))

    @pl.when(c == pl.num_programs(1) - 1)
    def _():
        y = acc_ref[...].T + alpha * x1_ref[...]
        o_ref[...] = _layer_norm(y, g_ref[...], b_ref[...])


def _peer_dense(x1t, u_b, vt_b, rank, cnt, ea, ebz, x1, g, b, alpha):
    seq, d_model = x1.shape
    n_exp = u_b.shape[0]
    tt = min(512, seq)
    ec = 1024
    once = pl.Buffered(1)
    gspec = pl.BlockSpec((PEER_HEADS, PEER_NKEYS, tt), lambda i, c: (0, 0, i), pipeline_mode=once)
    row = pl.BlockSpec((1, d_model), lambda i, c: (0, 0))
    return pl.pallas_call(
        functools.partial(_peer_dense_kernel, alpha=alpha, sub=ec // 2),
        grid=(seq // tt, n_exp // ec),
        in_specs=[pl.BlockSpec((d_model, tt), lambda i, c: (0, i), pipeline_mode=once),
                  pl.BlockSpec((ec, d_model), lambda i, c: (c, 0)),
                  pl.BlockSpec((d_model, ec), lambda i, c: (0, c)),
                  gspec, gspec, gspec, gspec,
                  pl.BlockSpec((tt, d_model), lambda i, c: (i, 0), pipeline_mode=once),
                  row, row],
        out_specs=pl.BlockSpec((tt, d_model), lambda i, c: (i, 0)),
        out_shape=jax.ShapeDtypeStruct((seq, d_model), F32),
        scratch_shapes=[pltpu.VMEM((d_model, tt), F32)],
        compiler_params=_params(2),
        name="peer_dense",
    )(x1t, u_b, vt_b, rank, cnt, ea, ebz, x1, g, b)


def _ple_kernel(x_ref, p_ref, wg_ref, wp_ref, g_ref, b_ref, o_ref, *, alpha):
    x = x_ref[...]
    gate = jax.nn.sigmoid(jnp.dot(x.astype(BF16), wg_ref[...], preferred_element_type=F32))
    emb = jnp.dot(p_ref[...].astype(BF16), wp_ref[...], preferred_element_type=F32)
    o_ref[...] = _layer_norm(alpha * x + gate * emb, g_ref[...], b_ref[...])


def _ple(x2, p2d, wg_b, wp_b, g, b, alpha):
    seq, d_model = x2.shape
    tm = min(512, seq)
    row = pl.BlockSpec((1, d_model), lambda i: (0, 0))
    return pl.pallas_call(
        functools.partial(_ple_kernel, alpha=alpha),
        grid=(seq // tm,),
        in_specs=[pl.BlockSpec((tm, d_model), lambda i: (i, 0)),
                  pl.BlockSpec((tm, p2d.shape[1]), lambda i: (i, 0)),
                  pl.BlockSpec(wg_b.shape, lambda i: (0, 0)),
                  pl.BlockSpec(wp_b.shape, lambda i: (0, 0)),
                  row, row],
        out_specs=pl.BlockSpec((tm, d_model), lambda i: (i, 0)),
        out_shape=jax.ShapeDtypeStruct((seq, d_model), F32),
        compiler_params=_params(1),
        name="ple",
    )(x2, p2d, wg_b, wp_b, g, b)


def kernel(x, p, w_in, w_out, lambda_q1, lambda_k1, lambda_q2, lambda_k2, diff_subln_g, ln_g, ln_b,
           peer_wq, peer_keys, peer_u, peer_v, ple_gate, ple_proj):
    batch, seq, d_model = x.shape
    assert batch == 1 and w_in.shape[0] == DEPTH
    alpha = (2.0 * DEPTH) ** 0.25
    x2d = x[0]
    for i in range(DEPTH):
        proj = _in_proj(x2d.astype(BF16), w_in[i].astype(BF16), seq)
        od = _diff_attn(proj, lambda_q1[i][None], lambda_k1[i][None], lambda_q2[i][None],
                        lambda_k2[i][None], diff_subln_g[i][None], seq, i)
        osb = _sb_attn(proj, seq)
        x1, x1t = _out_proj(od, osb, w_out[i].astype(BF16), x2d, ln_g[i, 0][None], ln_b[i, 0][None],
                            alpha)
        rank, cnt, ea, ebz = _peer_route(x1, peer_wq[i].astype(BF16), peer_keys[i].astype(BF16))
        x2 = _peer_dense(x1t, peer_u[i].astype(BF16), peer_v[i].T.astype(BF16), rank, cnt, ea, ebz, x1,
                         ln_g[i, 1][None], ln_b[i, 1][None], alpha)
        x2d = _ple(x2, p[i, 0], ple_gate[i].astype(BF16), ple_proj[i].astype(BF16),
                   ln_g[i, 2][None], ln_b[i, 2][None], alpha)
    return x2d[None]
```

```python
import functools
import math

import jax
import jax.numpy as jnp
import numpy as np
from jax import lax
from jax.experimental import pallas as pl
from jax.experimental.pallas import tpu as pltpu

F32 = jnp.float32
BF16 = jnp.bfloat16

LANES = 128
V7X_VMEM_BYTES = 64 * 1024 * 1024
VMEM_LIMIT = V7X_VMEM_BYTES * 7 // 8

DIFF_HEADS = 8
SB_HEADS = 8
HEAD_DIM = 128
DIFF_QK_DIM = HEAD_DIM // 2
ROT_DIM = DIFF_QK_DIM // 4
ROPE_THETA = 500000.0
PEER_HEADS = 8
PEER_NKEYS = 128
PEER_HALF = 128
PEER_TOPK = 16
LN_EPS = 1e-5
DEPTH = 1
ATTN_Q_TILE = 512
SB_K_TILE = 256
DIFF_K_TILE = 512
ATTN_GROUP = 4

NT_DIMS = (((1,), (1,)), ((), ()))


def _params(n_grid):
    return pltpu.CompilerParams(dimension_semantics=("arbitrary",) * n_grid,
                                vmem_limit_bytes=VMEM_LIMIT)


def _layer_norm(y, g, b):
    mu = jnp.mean(y, axis=-1, keepdims=True)
    yc = y - mu
    var = jnp.mean(yc * yc, axis=-1, keepdims=True)
    return yc * lax.rsqrt(var + LN_EPS) * g + b


def _wavefront(n_streams, stages):
    vals = [None] * n_streams
    for t in range(n_streams + len(stages) - 1):
        for j in reversed(range(len(stages))):
            s = t - j
            if 0 <= s < n_streams:
                vals[s] = stages[j](s, vals[s])
    return tuple(vals)


def _in_proj_kernel(x_ref, w_ref, c_ref, s1_ref, s2_ref, o_ref, xb_ref,
                    *, n_rope_tiles, sb_q_tile, sb_q_scale):
    j = pl.program_id(1)

    @pl.when(j == 0)
    def _():
        xb_ref[...] = x_ref[...].astype(BF16)

    def project(cols):
        return jnp.dot(xb_ref[...], w_ref[:, cols].astype(BF16), preferred_element_type=F32)

    @pl.when((j >= n_rope_tiles) & (j != sb_q_tile))
    def _():
        o_ref[...] = project(slice(None)).astype(o_ref.dtype)

    @pl.when(j == sb_q_tile)
    def _():
        o_ref[...] = (project(slice(None)) * sb_q_scale).astype(o_ref.dtype)

    @pl.when(j < n_rope_tiles)
    def _():
        c = c_ref[0]
        s1 = s1_ref[0]
        s2 = s2_ref[0]
        half = ROT_DIM // 2
        width = 2 * HEAD_DIM

        def rotate(k, acc):
            for hb in range(width // LANES):
                t = acc[:, hb * LANES:(hb + 1) * LANES]
                r = (t * c + pltpu.roll(t, half, axis=1) * s1
                     + pltpu.roll(t, LANES - half, axis=1) * s2)
                lo = k * width + hb * LANES
                o_ref[:, lo:lo + LANES] = r.astype(o_ref.dtype)

        _wavefront(o_ref.shape[1] // width,
                   (lambda k, _: project(slice(k * width, (k + 1) * width)), rotate))


def _rope_lane_tables(seq):
    half = ROT_DIM // 2
    lane = np.arange(LANES) % DIFF_QK_DIM
    holds_a = lane < half
    holds_b = (lane >= half) & (lane < ROT_DIM)
    pos = jnp.arange(seq, dtype=F32)
    inv = ROPE_THETA ** (-jnp.arange(0, ROT_DIM, 2, dtype=F32) / ROT_DIM)
    ang = pos[:, None] * inv[lane % half][None, :]
    cos, sin = jnp.cos(ang), jnp.sin(ang)
    tabs = (jnp.where(holds_a | holds_b, cos, 1.0), jnp.where(holds_b, sin, 0.0),
            jnp.where(holds_a, -sin, 0.0))
    qscale = DIFF_QK_DIM ** -0.5
    return [jnp.stack([t * qscale, t]) for t in tabs]


def _in_proj(x2d, w_in, seq):
    d_model = x2d.shape[1]
    n_cols = w_in.shape[1]
    tn = DIFF_HEADS * HEAD_DIM
    tm = min(1024, seq)
    c_tab, s1_tab, s2_tab = _rope_lane_tables(seq)
    tab_spec = pl.BlockSpec((1, tm, LANES), lambda i, j: (jnp.minimum(j, 1), i, 0))
    return pl.pallas_call(
        functools.partial(_in_proj_kernel, n_rope_tiles=2, sb_q_tile=3,
                          sb_q_scale=HEAD_DIM ** -0.5 * math.log2(math.e)),
        grid=(seq // tm, n_cols // tn),
        in_specs=[pl.BlockSpec((tm, d_model), lambda i, j: (i, 0)),
                  pl.BlockSpec((d_model, tn), lambda i, j: (0, j)),
                  tab_spec, tab_spec, tab_spec],
        out_specs=pl.BlockSpec((tm, tn), lambda i, j: (i, j)),
        out_shape=jax.ShapeDtypeStruct((seq, n_cols), BF16),
        scratch_shapes=[pltpu.VMEM((tm, d_model), BF16)],
        compiler_params=_params(2),
        name="in_proj",
    )(x2d, w_in, c_tab, s1_tab, s2_tab)


def _diff_attn_kernel(q_ref, k_ref, v_ref, lq1_ref, lk1_ref, lq2_ref, lk2_ref, g_ref, o_ref,
                      m_ref, l_ref, acc_ref, *, tq, tk, lambda_init, group):
    qi = pl.program_id(1)
    ratio = tq // tk
    lane = lax.broadcasted_iota(jnp.int32, (tq, HEAD_DIM), 1)
    qs = []
    for g in range(group):
        q = q_ref[:, g * HEAD_DIM:(g + 1) * HEAD_DIM]
        zero = jnp.zeros_like(q)
        qs.append(jnp.where(lane < DIFF_QK_DIM, q, zero))
        qs.append(jnp.where(lane < DIFF_QK_DIM, zero, q))
    row = lax.broadcasted_iota(jnp.int32, (tq, tk), 0)
    col = lax.broadcasted_iota(jnp.int32, (tq, tk), 1)

    n_streams = 2 * group

    def chunks(kis, valid):
        starts = [pl.multiple_of(ki * tk, tk) for ki in kis]

        def cols(w):
            head = w % n_streams // 2
            return slice(head * HEAD_DIM, (head + 1) * HEAD_DIM)

        def qk(w, _):
            return lax.dot_general(qs[w % n_streams], k_ref[pl.ds(starts[w // n_streams], tk), cols(w)],
                                   NT_DIMS, preferred_element_type=F32)

        def softmax(w, sc):
            s = w % n_streams
            if valid is not None:
                sc = jnp.where(valid, sc, -jnp.inf)
            m = m_ref[s]
            m_new = jnp.maximum(m, jnp.max(sc, axis=-1, keepdims=True))
            p = jnp.exp(sc - jnp.concatenate([m_new] * (tk // LANES), axis=1))
            a = jnp.exp(m - m_new)
            m_ref[s] = m_new
            l_ref[s] = a * l_ref[s] + jnp.sum(p, axis=-1, keepdims=True)
            return a, p.astype(BF16)

        def pv(w, x):
            a, p = x
            s = w % n_streams
            acc_ref[s] = a * acc_ref[s] + jnp.dot(p, v_ref[pl.ds(starts[w // n_streams], tk), cols(w)],
                                                  preferred_element_type=F32)

        _wavefront(len(kis) * n_streams, (qk, softmax, pv))

    m_ref[...] = jnp.full(m_ref.shape, -1e30, F32)
    l_ref[...] = jnp.zeros(l_ref.shape, F32)
    acc_ref[...] = jnp.zeros(acc_ref.shape, F32)
    for d in range(ratio):
        chunks([qi * ratio + d], col + d * tk <= row)

    def body(step, carry):
        chunks([step * ratio + d for d in range(ratio)], None)
        return carry

    lax.fori_loop(0, qi, body, 0)

    lam = (jnp.exp(jnp.sum(lq1_ref[...] * lk1_ref[...], axis=-1, keepdims=True))
           - jnp.exp(jnp.sum(lq2_ref[...] * lk2_ref[...], axis=-1, keepdims=True))
           + lambda_init)
    for g in range(group):
        od = acc_ref[2 * g] / l_ref[2 * g] - lam * (acc_ref[2 * g + 1] / l_ref[2 * g + 1])
        od = od * lax.rsqrt(jnp.mean(od * od, axis=-1, keepdims=True) + LN_EPS)
        od = od * g_ref[...] * (1.0 - lambda_init)
        o_ref[:, g * HEAD_DIM:(g + 1) * HEAD_DIM] = od.astype(o_ref.dtype)


def _diff_attn(proj, lq1, lk1, lq2, lk2, g, seq, layer):
    tq = min(ATTN_Q_TILE, seq)
    tk = min(DIFF_K_TILE, seq)
    gw = ATTN_GROUP * HEAD_DIM
    n_groups = DIFF_HEADS // ATTN_GROUP
    lambda_init = 0.8 - 0.6 * math.exp(-0.3 * layer)
    vec = pl.BlockSpec((1, DIFF_QK_DIM), lambda h, i: (0, 0))
    once = pl.Buffered(1)
    return pl.pallas_call(
        functools.partial(_diff_attn_kernel, tq=tq, tk=tk, lambda_init=lambda_init,
                          group=ATTN_GROUP),
        grid=(n_groups, seq // tq),
        in_specs=[pl.BlockSpec((tq, gw), lambda h, i: (i, h)),
                  pl.BlockSpec((seq, gw), lambda h, i: (0, n_groups + h), pipeline_mode=once),
                  pl.BlockSpec((seq, gw), lambda h, i: (0, 2 * n_groups + h), pipeline_mode=once),
                  vec, vec, vec, vec,
                  pl.BlockSpec((1, HEAD_DIM), lambda h, i: (0, 0))],
        out_specs=pl.BlockSpec((tq, gw), lambda h, i: (i, h)),
        out_shape=jax.ShapeDtypeStruct((seq, DIFF_HEADS * HEAD_DIM), BF16),
        scratch_shapes=[pltpu.VMEM((2 * ATTN_GROUP, tq, LANES), F32),
                        pltpu.VMEM((2 * ATTN_GROUP, tq, LANES), F32),
                        pltpu.VMEM((2 * ATTN_GROUP, tq, HEAD_DIM), F32)],
        compiler_params=_params(2),
        name="diff_attn",
    )(proj, proj, proj, lq1, lk1, lq2, lk2, g)


def _sb_attn_kernel(q_ref, k_ref, v_ref, o_ref, drop_ref, acc_ref, *, tq, tk, group):
    qi = pl.program_id(1)
    ratio = tq // tk
    tri = jnp.where(lax.broadcasted_iota(jnp.int32, (tk, tk), 0)
                    > lax.broadcasted_iota(jnp.int32, (tk, tk), 1), 1.0, 0.0).astype(BF16)
    row = lax.broadcasted_iota(jnp.int32, (tq, tk), 0)
    col = lax.broadcasted_iota(jnp.int32, (tq, tk), 1)

    def chunks(kis, valid):
        starts = [pl.multiple_of(ki * tk, tk) for ki in kis]

        def cols(w):
            return slice((w % group) * HEAD_DIM, (w % group + 1) * HEAD_DIM)

        def qk(w, _):
            return lax.dot_general(q_ref[:, cols(w)], k_ref[pl.ds(starts[w // group], tk), cols(w)],
                                   NT_DIMS, preferred_element_type=F32)

        def gates(w, z):
            g = w % group
            sp = jnp.log2(1.0 + jnp.exp2(-jnp.abs(z)))
            log_beta = jnp.minimum(z, 0.0) - sp
            skip = z - log_beta
            if valid is not None:
                skip = jnp.where(valid, skip, 0.0)
            drop = drop_ref[g]
            drop_ref[g] = drop + jnp.sum(skip, axis=-1, keepdims=True)
            return log_beta, drop, skip.astype(BF16)

        def cumsum(w, x):
            log_beta, drop, skip = x
            return log_beta, drop, jnp.dot(skip, tri, preferred_element_type=F32)

        def weights(w, x):
            log_beta, drop, suffix = x
            a = jnp.exp2(log_beta - suffix - jnp.concatenate([drop] * (tk // LANES), axis=1))
            if valid is not None:
                a = jnp.where(valid, a, 0.0)
            return a.astype(BF16)

        def pv(w, a):
            g = w % group
            acc_ref[g] += jnp.dot(a, v_ref[pl.ds(starts[w // group], tk), cols(w)],
                                  preferred_element_type=F32)

        _wavefront(len(kis) * group, (qk, gates, cumsum, weights, pv))

    drop_ref[...] = jnp.zeros(drop_ref.shape, F32)
    acc_ref[...] = jnp.zeros(acc_ref.shape, F32)
    for d in reversed(range(ratio)):
        chunks([qi * ratio + d], col + d * tk < row)

    def body(step, carry):
        chunks([(qi - step) * ratio - 1 - d for d in range(ratio)], None)
        return carry

    lax.fori_loop(0, qi, body, 0)
    for g in range(group):
        o_ref[:, g * HEAD_DIM:(g + 1) * HEAD_DIM] = acc_ref[g].astype(o_ref.dtype)


def _sb_attn(proj, seq):
    tq = min(ATTN_Q_TILE, seq)
    tk = min(SB_K_TILE, seq)
    gw = ATTN_GROUP * HEAD_DIM
    n_groups = SB_HEADS // ATTN_GROUP
    base = 3 * DIFF_HEADS // ATTN_GROUP
    once = pl.Buffered(1)
    return pl.pallas_call(
        functools.partial(_sb_attn_kernel, tq=tq, tk=tk, group=ATTN_GROUP),
        grid=(n_groups, seq // tq),
        in_specs=[pl.BlockSpec((tq, gw), lambda h, i: (i, base + h)),
                  pl.BlockSpec((seq, gw), lambda h, i: (0, base + n_groups + h), pipeline_mode=once),
                  pl.BlockSpec((seq, gw), lambda h, i: (0, base + 2 * n_groups + h),
                               pipeline_mode=once)],
        out_specs=pl.BlockSpec((tq, gw), lambda h, i: (i, h)),
        out_shape=jax.ShapeDtypeStruct((seq, SB_HEADS * HEAD_DIM), BF16),
        scratch_shapes=[pltpu.VMEM((ATTN_GROUP, tq, LANES), F32),
                        pltpu.VMEM((ATTN_GROUP, tq, HEAD_DIM), F32)],
        compiler_params=_params(2),
        name="sb_attn",
    )(proj, proj, proj)


def _out_proj_kernel(od_ref, osb_ref, w_ref, x_ref, g_ref, b_ref, o_ref, ot_ref, *, alpha):
    n_diff = od_ref.shape[1]
    y = jnp.dot(od_ref[...], w_ref[:n_diff, :], preferred_element_type=F32)
    y = y + jnp.dot(osb_ref[...], w_ref[n_diff:, :], preferred_element_type=F32)
    y = y + alpha * x_ref[...]
    out = _layer_norm(y, g_ref[...], b_ref[...])
    o_ref[...] = out
    ot_ref[...] = out.T.astype(ot_ref.dtype)


def _out_proj(od, osb, w_out_b, x2d, g, b, alpha):
    seq, d_model = x2d.shape
    tm = min(512, seq)
    row = pl.BlockSpec((1, d_model), lambda i: (0, 0))
    return pl.pallas_call(
        functools.partial(_out_proj_kernel, alpha=alpha),
        grid=(seq // tm,),
        in_specs=[pl.BlockSpec((tm, od.shape[1]), lambda i: (i, 0)),
                  pl.BlockSpec((tm, osb.shape[1]), lambda i: (i, 0)),
                  pl.BlockSpec(w_out_b.shape, lambda i: (0, 0)),
                  pl.BlockSpec((tm, d_model), lambda i: (i, 0)),
                  row, row],
        out_specs=[pl.BlockSpec((tm, d_model), lambda i: (i, 0)),
                   pl.BlockSpec((d_model, tm), lambda i: (0, i))],
        out_shape=[jax.ShapeDtypeStruct((seq, d_model), F32),
                   jax.ShapeDtypeStruct((d_model, seq), BF16)],
        compiler_params=_params(1),
        name="out_proj",
    )(od, osb, w_out_b, x2d, g, b)


def _top_values(s, k, with_rank=False):
    out = []
    rank = jnp.full(s.shape, float(k), F32)
    for a in range(k):
        m = jnp.max(s, axis=0, keepdims=True)
        out.append(m)
        hit = s == m
        s = jnp.where(hit, -jnp.inf, s)
        if with_rank:
            rank = jnp.where(hit, float(a), rank)
    return (out, rank) if with_rank else out


def _peer_route_kernel(x_ref, wq_ref, keys_ref, rank_ref, cnt_ref, ea_ref, ebz_ref):
    q = jnp.dot(x_ref[...].astype(BF16), wq_ref[...], preferred_element_type=F32).astype(BF16)
    half = PEER_TOPK // 2
    for h in range(PEER_HEADS):
        qa = q[:, (2 * h) * PEER_HALF:(2 * h + 1) * PEER_HALF]
        qb = q[:, (2 * h + 1) * PEER_HALF:(2 * h + 2) * PEER_HALF]
        sa = lax.dot_general(keys_ref[0], qa, NT_DIMS, preferred_element_type=F32)
        sb = lax.dot_general(keys_ref[1], qb, NT_DIMS, preferred_element_type=F32)
        va = _top_values(sa, PEER_TOPK)
        vb, rank = _top_values(sb, PEER_TOPK, with_rank=True)
        va_all = jnp.concatenate(va, axis=0)
        vb_all = jnp.concatenate(vb, axis=0)
        cand = jnp.concatenate([va[0] + vb_all]
                               + [va[a] + vb_all[:half] for a in range(1, half)]
                               + [va_all[half:] + vb[0]], axis=0)
        top = _top_values(cand, PEER_TOPK)
        tau = top[PEER_TOPK - 1]
        z = top[0] * 0.0
        for tv in top:
            z = z + jnp.exp(tv - top[0])
        sel = jnp.where(cand >= tau, 1.0, 0.0)
        cnt = jnp.zeros(sa.shape, F32)
        for a in range(PEER_TOPK):
            if a == 0:
                cnt_a = jnp.sum(sel[:PEER_TOPK], axis=0, keepdims=True)
            elif a < half:
                cnt_a = jnp.sum(sel[PEER_TOPK + (a - 1) * half:PEER_TOPK + a * half], axis=0,
                                keepdims=True)
            else:
                tail = PEER_TOPK + (half - 1) * half
                cnt_a = sel[tail + a - half:tail + a - half + 1]
            cnt = cnt + jnp.where(sa == va[a], cnt_a, 0.0)
        rank_ref[h] = rank.astype(rank_ref.dtype)
        cnt_ref[h] = cnt
        ea_ref[h] = jnp.exp(sa - va[0])
        ebz_ref[h] = (jnp.exp(sb - vb[0]) / z).astype(ebz_ref.dtype)


def _peer_route(x1, wq_b, keys_b):
    seq, d_model = x1.shape
    tt = min(256, seq)
    shape = (PEER_HEADS, PEER_NKEYS, seq)
    ospec = pl.BlockSpec((PEER_HEADS, PEER_NKEYS, tt), lambda i: (0, 0, i))
    return pl.pallas_call(
        _peer_route_kernel,
        grid=(seq // tt,),
        in_specs=[pl.BlockSpec((tt, d_model), lambda i: (i, 0)),
                  pl.BlockSpec(wq_b.shape, lambda i: (0, 0)),
                  pl.BlockSpec(keys_b.shape, lambda i: (0, 0, 0))],
        out_specs=[ospec] * 4,
        out_shape=[jax.ShapeDtypeStruct(shape, BF16), jax.ShapeDtypeStruct(shape, F32),
                   jax.ShapeDtypeStruct(shape, F32), jax.ShapeDtypeStruct(shape, BF16)],
        compiler_params=_params(1),
        name="peer_route",
    )(x1, wq_b, keys_b)


def _gelu_tanh(x):
    c = math.sqrt(2.0 / math.pi)
    return 0.5 * x * (1.0 + jnp.tanh(c * (x + 0.044715 * (x * x * x))))


def _bf16_rows(row, n):
    packed = jnp.broadcast_to(row, (16, row.shape[1])).astype(BF16)
    return jnp.concatenate([packed] * (n // 16), axis=0)


def _peer_dense_kernel(xt_ref, u_ref, vt_ref, rank_ref, cnt_ref, ea_ref, ebz_ref, x1_ref, g_ref, b_ref,
                       o_ref, acc_ref, *, alpha, sub):
    c = pl.program_id(1)
    n_sub = u_ref.shape[0] // sub
    groups = sub // PEER_NKEYS

    @pl.when(c == 0)
    def _():
        acc_ref[...] = jnp.zeros_like(acc_ref)

    def hidden(k, _):
        return jnp.dot(u_ref[k * sub:(k + 1) * sub, :], xt_ref[...], preferred_element_type=F32)

    def weights(k, ht):
        out = []
        for ii in range(groups):
            i = (c * n_sub + k) * groups + ii
            gate = jnp.zeros((PEER_NKEYS, ht.shape[1]), BF16)
            for h in range(PEER_HEADS):
                cnt_row = _bf16_rows(cnt_ref[h, pl.ds(i, 1), :], PEER_NKEYS)
                ea_row = _bf16_rows(ea_ref[h, pl.ds(i, 1), :], PEER_NKEYS)
                gate = gate + jnp.where(rank_ref[h] < cnt_row, ebz_ref[h], jnp.zeros_like(gate)) * ea_row
            act = _gelu_tanh(ht[ii * PEER_NKEYS:(ii + 1) * PEER_NKEYS])
            out.append(gate * act.astype(BF16))
        return jnp.concatenate(out, axis=0)

    def project(k, w):
        acc_ref[...] += jnp.dot(vt_ref[:, k * sub:(k + 1) * sub], w, preferred_element_type=F32)

    _wavefront(n_sub, (hidden, weights, project))

    @pl.when(c == pl.num_programs(1) - 1)
    def _():
        y = acc_ref[...].T + alpha * x1_ref[...]
        o_ref[...] = _layer_norm(y, g_ref[...], b_ref[...])


def _peer_dense(x1t, u_b, vt_b, rank, cnt, ea, ebz, x1, g, b, alpha):
    seq, d_model = x1.shape
    n_exp = u_b.shape[0]
    tt = min(512, seq)
    ec = 1024
    once = pl.Buffered(1)
    gspec = pl.BlockSpec((PEER_HEADS, PEER_NKEYS, tt), lambda i, c: (0, 0, i))
    row = pl.BlockSpec((1, d_model), lambda i, c: (0, 0))
    return pl.pallas_call(
        functools.partial(_peer_dense_kernel, alpha=alpha, sub=ec // 2),
        grid=(seq // tt, n_exp // ec),
        in_specs=[pl.BlockSpec((d_model, tt), lambda i, c: (0, i)),
                  pl.BlockSpec((ec, d_model), lambda i, c: (c, 0)),
                  pl.BlockSpec((d_model, ec), lambda i, c: (0, c)),
                  gspec, gspec, gspec, gspec,
                  pl.BlockSpec((tt, d_model), lambda i, c: (i, 0), pipeline_mode=once),
                  row, row],
        out_specs=pl.BlockSpec((tt, d_model), lambda i, c: (i, 0)),
        out_shape=jax.ShapeDtypeStruct((seq, d_model), F32),
        scratch_shapes=[pltpu.VMEM((d_model, tt), F32)],
        compiler_params=_params(2),
        name="peer_dense",
    )(x1t, u_b, vt_b, rank, cnt, ea, ebz, x1, g, b)


def _ple_kernel(x_ref, p_ref, wg_ref, wp_ref, g_ref, b_ref, o_ref, *, alpha):
    x = x_ref[...]
    gate = jax.nn.sigmoid(jnp.dot(x.astype(BF16), wg_ref[...], preferred_element_type=F32))
    emb = jnp.dot(p_ref[...].astype(BF16), wp_ref[...], preferred_element_type=F32)
    o_ref[...] = _layer_norm(alpha * x + gate * emb, g_ref[...], b_ref[...])


def _ple(x2, p2d, wg_b, wp_b, g, b, alpha):
    seq, d_model = x2.shape
    tm = min(512, seq)
    row = pl.BlockSpec((1, d_model), lambda i: (0, 0))
    return pl.pallas_call(
        functools.partial(_ple_kernel, alpha=alpha),
        grid=(seq // tm,),
        in_specs=[pl.BlockSpec((tm, d_model), lambda i: (i, 0)),
                  pl.BlockSpec((tm, p2d.shape[1]), lambda i: (i, 0)),
                  pl.BlockSpec(wg_b.shape, lambda i: (0, 0)),
                  pl.BlockSpec(wp_b.shape, lambda i: (0, 0)),
                  row, row],
        out_specs=pl.BlockSpec((tm, d_model), lambda i: (i, 0)),
        out_shape=jax.ShapeDtypeStruct((seq, d_model), F32),
        compiler_params=_params(1),
        name="ple",
    )(x2, p2d, wg_b, wp_b, g, b)


def kernel(x, p, w_in, w_out, lambda_q1, lambda_k1, lambda_q2, lambda_k2, diff_subln_g, ln_g, ln_b,
           peer_wq, peer_keys, peer_u, peer_v, ple_gate, ple_proj):
    batch, seq, d_model = x.shape
    assert batch == 1 and w_in.shape[0] == DEPTH
    alpha = (2.0 * DEPTH) ** 0.25
    x2d = x[0]
    for i in range(DEPTH):
        proj = _in_proj(x2d, w_in[i], seq)
        od = _diff_attn(proj, lambda_q1[i][None], lambda_k1[i][None], lambda_q2[i][None],
                        lambda_k2[i][None], diff_subln_g[i][None], seq, i)
        osb = _sb_attn(proj, seq)
        x1, x1t = _out_proj(od, osb, w_out[i].astype(BF16), x2d, ln_g[i, 0][None], ln_b[i, 0][None],
                            alpha)
        rank, cnt, ea, ebz = _peer_route(x1, peer_wq[i].astype(BF16), peer_keys[i].astype(BF16))
        x2 = _peer_dense(x1t, peer_u[i].astype(BF16), peer_v[i].T.astype(BF16), rank, cnt, ea, ebz, x1,
                         ln_g[i, 1][None], ln_b[i, 1][None], alpha)
        x2d = _ple(x2, p[i, 0], ple_gate[i].astype(BF16), ple_proj[i].astype(BF16),
                   ln_g[i, 2][None], ln_b[i, 2][None], alpha)
    return x2d[None]
```
